```python
import jax, jax.numpy as jnp
from jax import lax
import numpy as np

D_MODEL = 1024
BATCH = 2
SEQ = 8192
DEPTH = 1
DEC_BATCH = 32
DEC_SEQ = 1
PAST_LEN = 16384
PAGE_SIZE = 128

HEAD_DIM = 64
CONV_WIDTH = D_MODEL // 4
CONV_K = 3
FOX_WIDTH = D_MODEL // 2
FOX_HEADS = FOX_WIDTH // HEAD_DIM
MEM_WIDTH = D_MODEL // 4
MEM_HEADS = MEM_WIDTH // HEAD_DIM
MEM_LEN = 256
MIX_WIDTH = CONV_WIDTH + FOX_WIDTH + MEM_WIDTH
D_FF = -(-8 * D_MODEL // (3 * 256)) * 256
Q_BLOCK = 128
EPS = 1e-6
IN_SIZES = (CONV_WIDTH, CONV_WIDTH, CONV_WIDTH, FOX_WIDTH, FOX_WIDTH, FOX_WIDTH, FOX_HEADS, MEM_WIDTH)
IN_WIDTH = sum(IN_SIZES)

kernel_name = "hymba_conv_fox_memory_decode_step"


def _normalize(x):
    xf = x.astype(jnp.float32)
    return xf * lax.rsqrt(jnp.mean(xf * xf, axis=-1, keepdims=True) + EPS)


def rmsnorm(x, g):
    return (_normalize(x) * g.astype(jnp.float32)).astype(x.dtype)


def split_proj(h, w_in, b_f):
    p = h @ w_in
    offs = [int(o) for o in np.cumsum(IN_SIZES)[:-1]]
    b_c, c_c, h_c, q, k, v, f_logit, q_m = jnp.split(p, offs, axis=-1)
    B, T = h.shape[0], h.shape[1]
    q = q.reshape(B, T, FOX_HEADS, HEAD_DIM)
    k = k.reshape(B, T, FOX_HEADS, HEAD_DIM)
    v = v.reshape(B, T, FOX_HEADS, HEAD_DIM)
    q_m = q_m.reshape(B, T, MEM_HEADS, HEAD_DIM)
    logf = jax.nn.log_sigmoid((f_logit + b_f).astype(jnp.float32))
    return b_c, c_c, h_c, q, k, v, logf, q_m


def causal_conv(buf, conv_w, n_out):
    out = conv_w[0] * buf[:, 0:n_out]
    for i in range(1, CONV_K):
        out = out + conv_w[i] * buf[:, i:i + n_out]
    return out


def fox_prompt(q, k, v, logf):
    B, S, H, Dh = q.shape
    scale = Dh ** -0.5
    c = jnp.cumsum(logf, axis=1)
    ck = jnp.transpose(c, (0, 2, 1))
    n_blk = S // Q_BLOCK
    qb = q.reshape(B, n_blk, Q_BLOCK, H, Dh).transpose(1, 0, 2, 3, 4)
    cqb = ck.reshape(B, H, n_blk, Q_BLOCK).transpose(2, 0, 1, 3)
    kpos = jnp.arange(S)

    def block(args):
        qi, cqi, bi = args
        s = jnp.einsum('bqhd,bkhd->bhqk', qi, k).astype(jnp.float32) * scale
        s = s + cqi[..., :, None] - ck[:, :, None, :]
        qpos = bi * Q_BLOCK + jnp.arange(Q_BLOCK)
        mask = kpos[None, :] <= qpos[:, None]
        p = jax.nn.softmax(jnp.where(mask, s, -jnp.inf), axis=-1)
        return jnp.einsum('bhqk,bkhd->bqhd', p.astype(v.dtype), v)

    o = lax.map(block, (qb, cqb, jnp.arange(n_blk)))
    return o.transpose(1, 0, 2, 3, 4).reshape(B, S, H * Dh)


def fox_sample(q, k_new, v_new, logf_new, k_past, v_past, logf_past):
    B, T, H, Dh = q.shape
    P = k_past.shape[1]
    scale = Dh ** -0.5
    k_all = jnp.concatenate([k_past.astype(k_new.dtype), k_new], axis=1)
    v_all = jnp.concatenate([v_past.astype(v_new.dtype), v_new], axis=1)
    lf_all = jnp.concatenate([logf_past.astype(jnp.float32), logf_new], axis=1)
    c = jnp.transpose(jnp.cumsum(lf_all, axis=1), (0, 2, 1))
    cq = c[:, :, P:]
    s = jnp.einsum('bqhd,bkhd->bhqk', q, k_all).astype(jnp.float32) * scale
    s = s + cq[..., :, None] - c[:, :, None, :]
    kpos = jnp.arange(P + T)
    qpos = P + jnp.arange(T)
    mask = kpos[None, :] <= qpos[:, None]
    p = jax.nn.softmax(jnp.where(mask, s, -jnp.inf), axis=-1)
    o = jnp.einsum('bhqk,bkhd->bqhd', p.astype(v_all.dtype), v_all)
    return o.reshape(B, T, H * Dh)


def mem_attention(q_m, mem_k, mem_v):
    B, T = q_m.shape[0], q_m.shape[1]
    s = jnp.einsum('bqhd,bmhd->bhqm', q_m, mem_k.astype(q_m.dtype)).astype(jnp.float32) * (HEAD_DIM ** -0.5)
    p = jax.nn.softmax(s, axis=-1)
    o = jnp.einsum('bhqm,bmhd->bqhd', p.astype(q_m.dtype), mem_v.astype(q_m.dtype))
    return o.reshape(B, T, MEM_WIDTH)


def mem_kv(mem, mem_norm_g, w_mem_kv):
    kv = rmsnorm(mem, mem_norm_g) @ w_mem_kv
    mk, mv = jnp.split(kv, [MEM_WIDTH], axis=-1)
    B = mem.shape[0]
    return mk.reshape(B, MEM_LEN, MEM_HEADS, HEAD_DIM), mv.reshape(B, MEM_LEN, MEM_HEADS, HEAD_DIM)


def mix_out(y_conv, y_fox, y_mem, out_norm_g, w_out):
    y = jnp.concatenate([_normalize(y_conv), _normalize(y_fox), _normalize(y_mem)], axis=-1)
    y = (y * out_norm_g.astype(jnp.float32)).astype(y_conv.dtype)
    return y @ w_out


def swiglu(h, w_gate, w_up, w_down):
    return (jax.nn.silu(h @ w_gate) * (h @ w_up)) @ w_down


def setup_inputs(seed: int = 0) -> dict:
    key = jax.random.key(seed)
    ks = jax.random.split(key, 24)
    n_pages = PAST_LEN // PAGE_SIZE
    n_pool = (DEC_BATCH * n_pages * 5) // 4
    f32 = jnp.float32
    nrm = lambda k, shape, s=1.0: jax.random.normal(k, shape, f32) * s
    page_table = jax.random.permutation(ks[0], n_pool)[:DEC_BATCH * n_pages].reshape(DEC_BATCH, n_pages).astype(jnp.int32)
    return {
        "x_prompt": nrm(ks[1], (BATCH, SEQ, D_MODEL)),
        "x_sample": nrm(ks[2], (DEC_BATCH, DEC_SEQ, D_MODEL)),
        "mem_prompt": nrm(ks[3], (BATCH, MEM_LEN, D_MODEL)),
        "cache_k": nrm(ks[4], (DEPTH, n_pool, PAGE_SIZE, FOX_HEADS, HEAD_DIM)),
        "cache_v": nrm(ks[5], (DEPTH, n_pool, PAGE_SIZE, FOX_HEADS, HEAD_DIM)),
        "cache_logf": jax.nn.log_sigmoid(2.0 + nrm(ks[6], (DEPTH, n_pool, PAGE_SIZE, FOX_HEADS), 1.0)),
        "state_conv": nrm(ks[7], (DEPTH, DEC_BATCH, CONV_K - 1, CONV_WIDTH)),
        "cache_mem_k": nrm(ks[8], (DEPTH, DEC_BATCH, MEM_LEN, MEM_HEADS, HEAD_DIM)),
        "cache_mem_v": nrm(ks[9], (DEPTH, DEC_BATCH, MEM_LEN, MEM_HEADS, HEAD_DIM)),
        "page_table": page_table,
        "norm1_g": 1.0 + nrm(ks[10], (DEPTH, D_MODEL), 0.1),
        "w_in": nrm(ks[11], (DEPTH, D_MODEL, IN_WIDTH), D_MODEL ** -0.5),
        "b_f": 2.0 + nrm(ks[12], (DEPTH, FOX_HEADS), 0.5),
        "conv_w": nrm(ks[13], (DEPTH, CONV_K, CONV_WIDTH), CONV_K ** -0.5),
        "mem_norm_g": 1.0 + nrm(ks[14], (DEPTH, D_MODEL), 0.1),
        "w_mem_kv": nrm(ks[15], (DEPTH, D_MODEL, 2 * MEM_WIDTH), D_MODEL ** -0.5),
        "out_norm_g": 1.0 + nrm(ks[16], (DEPTH, MIX_WIDTH), 0.1),
        "w_out": nrm(ks[17], (DEPTH, MIX_WIDTH, D_MODEL), MIX_WIDTH ** -0.5),
        "norm2_g": 1.0 + nrm(ks[18], (DEPTH, D_MODEL), 0.1),
        "w_gate": nrm(ks[19], (DEPTH, D_MODEL, D_FF), D_MODEL ** -0.5),
        "w_up": nrm(ks[20], (DEPTH, D_MODEL, D_FF), D_MODEL ** -0.5),
        "w_down": nrm(ks[21], (DEPTH, D_FF, D_MODEL), D_FF ** -0.5),
        "final_norm_g": 1.0 + nrm(ks[22], (D_MODEL,), 0.1),
    }


def reference(x_prompt, x_sample, mem_prompt, cache_k, cache_v, cache_logf, state_conv, cache_mem_k, cache_mem_v,
              page_table, norm1_g, w_in, b_f, conv_w, mem_norm_g, w_mem_kv, out_norm_g, w_out, norm2_g,
              w_gate, w_up, w_down, final_norm_g):
    xp, xs = x_prompt, x_sample
    n_past = page_table.shape[1] * cache_k.shape[2]
    nk_p, nv_p, nlf_p, nconv_p, nmk_p, nmv_p = [], [], [], [], [], []
    nk_s, nv_s, nlf_s, nconv_s = [], [], [], []
    for l in range(DEPTH):
        hp = rmsnorm(xp, norm1_g[l])
        b_c, c_c, h_c, q, k, v, logf, q_m = split_proj(hp, w_in[l], b_f[l])
        u = c_c * h_c
        buf = jnp.pad(u, ((0, 0), (CONV_K - 1, 0), (0, 0)))
        y_conv = b_c * causal_conv(buf, conv_w[l], xp.shape[1])
        y_fox = fox_prompt(q, k, v, logf)
        mk, mv = mem_kv(mem_prompt, mem_norm_g[l], w_mem_kv[l])
        y_mem = mem_attention(q_m, mk, mv)
        xp = xp + mix_out(y_conv, y_fox, y_mem, out_norm_g[l], w_out[l])
        xp = xp + swiglu(rmsnorm(xp, norm2_g[l]), w_gate[l], w_up[l], w_down[l])
        nk_p.append(k); nv_p.append(v); nlf_p.append(logf)
        nconv_p.append(buf[:, -(CONV_K - 1):]); nmk_p.append(mk); nmv_p.append(mv)

        hs = rmsnorm(xs, norm1_g[l])
        b_c, c_c, h_c, q, k, v, logf, q_m = split_proj(hs, w_in[l], b_f[l])
        u = c_c * h_c
        buf = jnp.concatenate([state_conv[l].astype(u.dtype), u], axis=1)
        y_conv = b_c * causal_conv(buf, conv_w[l], xs.shape[1])
        k_past = cache_k[l, page_table].reshape(xs.shape[0], n_past, FOX_HEADS, HEAD_DIM)
        v_past = cache_v[l, page_table].reshape(xs.shape[0], n_past, FOX_HEADS, HEAD_DIM)
        lf_past = cache_logf[l, page_table].reshape(xs.shape[0], n_past, FOX_HEADS)
        y_fox = fox_sample(q, k, v, logf, k_past, v_past, lf_past)
        y_mem = mem_attention(q_m, cache_mem_k[l], cache_mem_v[l])
        xs = xs + mix_out(y_conv, y_fox, y_mem, out_norm_g[l], w_out[l])
        xs = xs + swiglu(rmsnorm(xs, norm2_g[l]), w_gate[l], w_up[l], w_down[l])
        nk_s.append(k); nv_s.append(v); nlf_s.append(logf); nconv_s.append(buf[:, -(CONV_K - 1):])

    y_prompt = rmsnorm(xp, final_norm_g)
    y_sample = rmsnorm(xs, final_norm_g)
    return (y_prompt, y_sample,
            jnp.stack(nk_p), jnp.stack(nv_p), jnp.stack(nlf_p), jnp.stack(nconv_p),
            jnp.stack(nmk_p), jnp.stack(nmv_p),
            jnp.stack(nk_s), jnp.stack(nv_s), jnp.stack(nlf_s), jnp.stack(nconv_s))
```

```python
import functools

import numpy as np
import jax
import jax.numpy as jnp
from jax import lax
from jax.experimental import pallas as pl
from jax.experimental.pallas import tpu as pltpu

F32 = jnp.float32
BF16 = jnp.bfloat16
EPS = 1e-6

LANES = 128
SUBLANES = 8
ROW_TILE = 512
PAGES_PER_STEP = 8
VMEM_LIMIT = 56 * 1024 * 1024


def _dot(a, b):
    return jnp.dot(a, b, preferred_element_type=F32)


def _dot_nt(a, b):
    return lax.dot_general(a, b, (((1,), (1,)), ((), ())), preferred_element_type=F32)


def _normalize(x):
    return x * lax.rsqrt(jnp.mean(x * x, axis=-1, keepdims=True) + EPS)


def _split3(x):
    hi = x.astype(BF16)
    r = x - hi.astype(F32)
    mid = r.astype(BF16)
    lo = (r - mid.astype(F32)).astype(BF16)
    return hi, mid, lo


def _dot_exact_lhs(x, m):
    hi, mid, lo = _split3(x)
    return _dot(hi, m) + _dot(mid, m) + _dot(lo, m)


def _log_sigmoid(x):
    return jnp.minimum(x, 0.0) - jnp.log1p(jnp.exp(-jnp.abs(x)))


def _inproj_prompt_kernel(x_ref, g_ref, w_ref, bf_ref, cw_ref, selq_ref, selk_ref,
                          yconv_ref, qaug_ref, kaug_ref, k_ref, v_ref, vb_ref, logf_ref, qm_ref, utail_ref,
                          ucarry, ccarry, *, cw_width, fox_width, mem_width, n_heads, scale):
    t = pl.program_id(1)
    tm = x_ref.shape[1]

    @pl.when(t == 0)
    def _():
        ucarry[...] = jnp.zeros_like(ucarry)
        ccarry[...] = jnp.zeros_like(ccarry)

    x = x_ref[0]
    xn = (_normalize(x) * g_ref[...]).astype(BF16)

    o_q = 3 * cw_width
    o_k = o_q + fox_width
    o_v = o_k + fox_width
    o_m = o_v + fox_width
    o_f = o_m + mem_width

    pc = _dot(xn, w_ref[:, 0:o_q])
    b_c = pc[:, 0:cw_width]
    u = pc[:, cw_width:2 * cw_width] * pc[:, 2 * cw_width:3 * cw_width]
    prev = ucarry[...]
    row8 = lax.broadcasted_iota(jnp.int32, (SUBLANES, cw_width), 0)
    r1 = pltpu.roll(u, 1, 0)
    r2 = pltpu.roll(u, 2, 0)
    h1 = jnp.where(row8 < 1, pltpu.roll(prev, 1, 0), r1[0:SUBLANES])
    h2 = jnp.where(row8 < 2, pltpu.roll(prev, 2, 0), r2[0:SUBLANES])
    u1 = jnp.concatenate([h1, r1[SUBLANES:]], axis=0)
    u2 = jnp.concatenate([h2, r2[SUBLANES:]], axis=0)
    cw = cw_ref[...]
    yconv_ref[0] = b_c * (cw[0:1] * u2 + cw[1:2] * u1 + cw[2:3] * u)
    tail = u[tm - SUBLANES:]
    ucarry[...] = tail
    utail_ref[0] = tail

    pf = _dot(xn, w_ref[:, o_m:o_f + LANES])
    qm_ref[0] = (pf[:, 0:mem_width] * scale).astype(BF16)
    logf = _log_sigmoid(pf[:, mem_width:mem_width + LANES] + bf_ref[...])
    logf_ref[0] = logf[:, 0:n_heads]

    lane = lax.broadcasted_iota(jnp.int32, (tm, LANES), 1)
    row = lax.broadcasted_iota(jnp.int32, (tm, LANES), 0)
    c = jnp.where(lane < n_heads, logf, 0.0)
    sh = 1
    while sh < tm:
        c = c + jnp.where(row >= sh, pltpu.roll(c, sh, 0), 0.0)
        sh *= 2
    c = c + ccarry[0:1]
    ccarry[...] = jnp.broadcast_to(c[tm - 1:tm], ccarry.shape)

    hi = c.astype(BF16).astype(F32)
    r = c - hi
    mid = r.astype(BF16).astype(F32)
    lo = r - mid
    parts = (hi + pltpu.roll(mid, n_heads, 1) + pltpu.roll(lo, 2 * n_heads, 1)
             + jnp.where(lane == 3 * n_heads, 1.0, 0.0)).astype(BF16)
    eq = _dot(parts, selq_ref[...])
    ek = _dot(parts, selk_ref[...])

    pq = _dot(xn, w_ref[:, o_q:o_k]) * scale
    pk = _dot(xn, w_ref[:, o_k:o_v])
    pv = _dot(xn, w_ref[:, o_v:o_m])
    k_ref[0] = pk
    v_ref[0] = pv
    vb_ref[0] = pv.astype(BF16)
    half = lane < (LANES // 2)
    for hp in range(n_heads // 2):
        sl = slice(hp * LANES, (hp + 1) * LANES)
        for src, ext, dst in ((pq, eq, qaug_ref), (pk, ek, kaug_ref)):
            pair = src[:, sl]
            rot = pltpu.roll(pair, LANES // 2, 1)
            e0 = ext[:, (2 * hp) * LANES:(2 * hp + 1) * LANES]
            e1 = ext[:, (2 * hp + 1) * LANES:(2 * hp + 2) * LANES]
            dst[0, 2 * hp] = jnp.where(half, pair, e0).astype(BF16)
            dst[0, 2 * hp + 1] = jnp.where(half, rot, e1).astype(BF16)


def _sel_matrices(n_heads, head_dim):
    selq = np.zeros((LANES, n_heads * LANES), np.float32)
    selk = np.zeros((LANES, n_heads * LANES), np.float32)
    for h in range(n_heads):
        base = h * LANES + head_dim
        for part in range(3):
            selq[part * n_heads + h, base + part] = 1.0
            selq[3 * n_heads, base + 3 + part] = 1.0
            selk[3 * n_heads, base + part] = 1.0
            selk[part * n_heads + h, base + 3 + part] = -1.0
    return jnp.asarray(selq, BF16), jnp.asarray(selk, BF16)


def _inproj_prompt(x, g, w, bf, cw, selq, selk, *, n_heads, head_dim, mem_width):
    B, S, D = x.shape
    cw_width = cw.shape[1]
    fox_width = n_heads * head_dim
    tm = ROW_TILE
    nt = S // tm
    kern = functools.partial(_inproj_prompt_kernel, cw_width=cw_width, fox_width=fox_width,
                             mem_width=mem_width, n_heads=n_heads, scale=head_dim ** -0.5)
    const = lambda shape: pl.BlockSpec(shape, lambda b, t: (0,) * len(shape))
    row_spec = lambda width: pl.BlockSpec((1, tm, width), lambda b, t: (b, t, 0))
    head_spec = pl.BlockSpec((1, n_heads, tm, LANES), lambda b, t: (b, 0, t, 0))
    return pl.pallas_call(
        kern,
        grid=(B, nt),
        in_specs=[row_spec(D), const(g.shape), const(w.shape), const(bf.shape), const(cw.shape),
                  const(selq.shape), const(selk.shape)],
        out_specs=[row_spec(cw_width), head_spec, head_spec, row_spec(fox_width), row_spec(fox_width),
                   row_spec(fox_width), row_spec(n_heads), row_spec(mem_width),
                   pl.BlockSpec((1, SUBLANES, cw_width), lambda b, t: (b, 0, 0))],
        out_shape=[jax.ShapeDtypeStruct((B, S, cw_width), F32),
                   jax.ShapeDtypeStruct((B, n_heads, S, LANES), BF16),
                   jax.ShapeDtypeStruct((B, n_heads, S, LANES), BF16),
                   jax.ShapeDtypeStruct((B, S, fox_width), F32),
                   jax.ShapeDtypeStruct((B, S, fox_width), F32),
                   jax.ShapeDtypeStruct((B, S, fox_width), BF16),
                   jax.ShapeDtypeStruct((B, S, n_heads), F32),
                   jax.ShapeDtypeStruct((B, S, mem_width), BF16),
                   jax.ShapeDtypeStruct((B, SUBLANES, cw_width), F32)],
        scratch_shapes=[pltpu.VMEM((SUBLANES, cw_width), F32), pltpu.VMEM((SUBLANES, LANES), F32)],
        compiler_params=pltpu.CompilerParams(dimension_semantics=("arbitrary", "arbitrary"),
                                             vmem_limit_bytes=VMEM_LIMIT),
    )(x, g, w, bf, cw, selq, selk)


def _fox_prompt_kernel(q_ref, k_ref, v_ref, o_ref, m_sc, l_sc, acc_sc):
    i = pl.program_id(2)
    tq = q_ref.shape[2]
    tk = tq
    m_sc[...] = jnp.full_like(m_sc, -jnp.inf)
    l_sc[...] = jnp.zeros_like(l_sc)
    acc_sc[...] = jnp.zeros_like(acc_sc)

    def block(j, masked):
        off = pl.multiple_of(j * tk, tk)
        vj = v_ref[0, pl.ds(off, tk), :]
        for hh in range(2):
            s = _dot_nt(q_ref[0, hh], k_ref[0, hh, pl.ds(off, tk), :])
            if masked:
                qpos = lax.broadcasted_iota(jnp.int32, s.shape, 0)
                kpos = lax.broadcasted_iota(jnp.int32, s.shape, 1)
                s = jnp.where(kpos <= qpos, s, -jnp.inf)
            m_old = m_sc[hh]
            m_new = jnp.maximum(m_old, jnp.max(s, axis=-1, keepdims=True))
            alpha = jnp.exp(m_old - m_new)
            p = jnp.exp(s - m_new)
            l_sc[hh] = alpha * l_sc[hh] + jnp.sum(p, axis=-1, keepdims=True)
            acc_sc[hh] = alpha * acc_sc[hh] + _dot(p.astype(BF16), vj)
            m_sc[hh] = m_new

    def body(j, carry):
        block(j, False)
        return carry

    lax.fori_loop(0, i, body, 0)
    block(i, True)
    lane = lax.broadcasted_iota(jnp.int32, (tq, LANES), 1)
    o_ref[0] = jnp.where(lane < LANES // 2, acc_sc[0] / l_sc[0], acc_sc[1] / l_sc[1])


def _fox_prompt(qaug, kaug, vb):
    B, H, S, _ = qaug.shape
    tq = ROW_TILE
    return pl.pallas_call(
        _fox_prompt_kernel,
        grid=(B, H // 2, S // tq),
        in_specs=[pl.BlockSpec((1, 2, tq, LANES), lambda b, hp, i: (b, hp, i, 0)),
                  pl.BlockSpec((1, 2, S, LANES), lambda b, hp, i: (b, hp, 0, 0)),
                  pl.BlockSpec((1, S, LANES), lambda b, hp, i: (b, 0, hp))],
        out_specs=pl.BlockSpec((1, tq, LANES), lambda b, hp, i: (b, i, hp)),
        out_shape=jax.ShapeDtypeStruct((B, S, vb.shape[2]), F32),
        scratch_shapes=[pltpu.VMEM((2, tq, 1), F32), pltpu.VMEM((2, tq, 1), F32),
                        pltpu.VMEM((2, tq, LANES), F32)],
        compiler_params=pltpu.CompilerParams(dimension_semantics=("arbitrary", "arbitrary", "arbitrary"),
                                             vmem_limit_bytes=VMEM_LIMIT),
    )(qaug, kaug, vb)


def _mem_kv_kernel(mem_ref, g_ref, w_ref, mk_ref, mv_ref):
    h = (_normalize(mem_ref[0]) * g_ref[...]).astype(BF16)
    kv = _dot(h, w_ref[...])
    width = mk_ref.shape[2]
    mk_ref[0] = kv[:, 0:width]
    mv_ref[0] = kv[:, width:2 * width]


def _mem_kv(mem, g, w):
    B, M, D = mem.shape
    width = w.shape[1] // 2
    out = jax.ShapeDtypeStruct((B, M, width), F32)
    return pl.pallas_call(
        _mem_kv_kernel,
        grid=(B,),
        in_specs=[pl.BlockSpec((1, M, D), lambda b: (b, 0, 0)),
                  pl.BlockSpec(g.shape, lambda b: (0, 0)),
                  pl.BlockSpec(w.shape, lambda b: (0, 0))],
        out_specs=[pl.BlockSpec((1, M, width), lambda b: (b, 0, 0))] * 2,
        out_shape=[out, out],
        compiler_params=pltpu.CompilerParams(dimension_semantics=("arbitrary",)),
    )(mem, g, w)


def _mem_attn_prompt_kernel(q_ref, mk_ref, mv_ref, o_ref):
    tm = q_ref.shape[1]
    n_pairs = q_ref.shape[2] // LANES
    lane_k = lax.broadcasted_iota(jnp.int32, (mk_ref.shape[1], LANES), 1)
    lane_o = lax.broadcasted_iota(jnp.int32, (tm, LANES), 1)
    for hp in range(n_pairs):
        sl = slice(hp * LANES, (hp + 1) * LANES)
        q = q_ref[0, :, sl]
        mk = mk_ref[0, :, sl]
        mv = mv_ref[0, :, sl].astype(BF16)
        outs = []
        for hh in range(2):
            in_head = (lane_k < LANES // 2) if hh == 0 else (lane_k >= LANES // 2)
            s = _dot_nt(q, jnp.where(in_head, mk, 0.0).astype(BF16))
            p = jnp.exp(s - jnp.max(s, axis=-1, keepdims=True))
            l = jnp.sum(p, axis=-1, keepdims=True)
            outs.append(_dot(p.astype(BF16), mv) / l)
        o_ref[0, :, sl] = jnp.where(lane_o < LANES // 2, outs[0], outs[1])


def _mem_attn_prompt(qm, mk, mv):
    B, S, W = qm.shape
    M = mk.shape[1]
    tm = ROW_TILE
    return pl.pallas_call(
        _mem_attn_prompt_kernel,
        grid=(B, S // tm),
        in_specs=[pl.BlockSpec((1, tm, W), lambda b, t: (b, t, 0)),
                  pl.BlockSpec((1, M, W), lambda b, t: (b, 0, 0)),
                  pl.BlockSpec((1, M, W), lambda b, t: (b, 0, 0))],
        out_specs=pl.BlockSpec((1, tm, W), lambda b, t: (b, t, 0)),
        out_shape=jax.ShapeDtypeStruct((B, S, W), F32),
        compiler_params=pltpu.CompilerParams(dimension_semantics=("arbitrary", "arbitrary")),
    )(qm, mk, mv)


def _post_kernel(x_ref, yc_ref, yf_ref, ym_ref, og_ref, wo_ref, n2_ref, wg_ref, wu_ref, wd_ref, fg_ref, o_ref,
                 *, ff_chunks):
    og = og_ref[...]
    wc = yc_ref.shape[1]
    wf = yf_ref.shape[1]
    wm = ym_ref.shape[1]
    a = (_normalize(yc_ref[...]) * og[:, 0:wc]).astype(BF16)
    b = (_normalize(yf_ref[...]) * og[:, wc:wc + wf]).astype(BF16)
    c = (_normalize(ym_ref[...]) * og[:, wc + wf:wc + wf + wm]).astype(BF16)
    mix = (_dot(a, wo_ref[0:wc, :]) + _dot(b, wo_ref[wc:wc + wf, :])
           + _dot(c, wo_ref[wc + wf:wc + wf + wm, :]))
    x1 = x_ref[...] + mix
    h = (_normalize(x1) * n2_ref[...]).astype(BF16)
    d_ff = wg_ref.shape[1]
    step = d_ff // ff_chunks
    ffn = jnp.zeros_like(x1)
    for ci in range(ff_chunks):
        sl = slice(ci * step, (ci + 1) * step)
        gate = _dot(h, wg_ref[:, sl])
        up = _dot(h, wu_ref[:, sl])
        act = (gate * jax.nn.sigmoid(gate) * up).astype(BF16)
        ffn = ffn + _dot(act, wd_ref[sl, :])
    x2 = x1 + ffn
    o_ref[...] = _normalize(x2) * fg_ref[...]


def _post(x, yc, yf, ym, og, wo, n2, wg, wu, wd, fg, *, tm):
    N, D = x.shape
    d_ff = wg.shape[1]
    ff_chunks = 2 if (d_ff % (2 * LANES) == 0) else 1
    const = lambda a: pl.BlockSpec(a.shape, lambda t: (0, 0), pipeline_mode=pl.Buffered(1))
    rows = lambda a: pl.BlockSpec((tm, a.shape[1]), lambda t: (t, 0))
    return pl.pallas_call(
        functools.partial(_post_kernel, ff_chunks=ff_chunks),
        grid=(N // tm,),
        in_specs=[rows(x), rows(yc), rows(yf), rows(ym), const(og), const(wo), const(n2), const(wg), const(wu),
                  const(wd), const(fg)],
        out_specs=pl.BlockSpec((tm, D), lambda t: (t, 0)),
        out_shape=jax.ShapeDtypeStruct((N, D), F32),
        compiler_params=pltpu.CompilerParams(dimension_semantics=("arbitrary",), vmem_limit_bytes=VMEM_LIMIT),
    )(x, yc, yf, ym, og, wo, n2, wg, wu, wd, fg)


def _inproj_sample_kernel(x_ref, g_ref, w_ref, bf_ref, cw_ref, h0_ref, h1_ref,
                          yconv_ref, u_ref, q_ref, k_ref, v_ref, logf_ref, qm_ref,
                          *, cw_width, fox_width, mem_width, scale):
    xn = (_normalize(x_ref[...]) * g_ref[...]).astype(BF16)
    p = _dot(xn, w_ref[...])
    o_q = 3 * cw_width
    o_k = o_q + fox_width
    o_v = o_k + fox_width
    o_m = o_v + fox_width
    o_f = o_m + mem_width
    b_c = p[:, 0:cw_width]
    u = p[:, cw_width:2 * cw_width] * p[:, 2 * cw_width:3 * cw_width]
    cw = cw_ref[...]
    yconv_ref[...] = b_c * (cw[0:1] * h0_ref[...] + cw[1:2] * h1_ref[...] + cw[2:3] * u)
    u_ref[...] = u
    q_ref[...] = p[:, o_q:o_k] * scale
    k_ref[...] = p[:, o_k:o_v]
    v_ref[...] = p[:, o_v:o_m]
    qm_ref[...] = p[:, o_m:o_f] * scale
    logf_ref[...] = _log_sigmoid(p[:, o_f:o_f + LANES] + bf_ref[...])


def _inproj_sample(x, g, w, bf, cw, h0, h1, *, n_heads, head_dim, mem_width):
    N, D = x.shape
    cw_width = cw.shape[1]
    fox_width = n_heads * head_dim
    kern = functools.partial(_inproj_sample_kernel, cw_width=cw_width, fox_width=fox_width,
                             mem_width=mem_width, scale=head_dim ** -0.5)
    sds = lambda width: jax.ShapeDtypeStruct((N, width), F32)
    return pl.pallas_call(
        kern,
        out_shape=[sds(cw_width), sds(cw_width), sds(fox_width), sds(fox_width), sds(fox_width), sds(LANES),
                   sds(mem_width)],
        compiler_params=pltpu.CompilerParams(vmem_limit_bytes=VMEM_LIMIT),
    )(x, g, w, bf, cw, h0, h1)


NEG_BIG = -1e30


def _tile_sum(x):
    return jnp.sum(x.reshape(x.shape[0] // SUBLANES, SUBLANES, x.shape[1]), axis=0)


def _sublane_allsum(x):
    x = x + pltpu.roll(x, 4, 0)
    x = x + pltpu.roll(x, 2, 0)
    return x + pltpu.roll(x, 1, 0)


def _scale_cols(x, w):
    r, l = x.shape
    return (x.reshape(r // SUBLANES, SUBLANES, l) * w[None]).reshape(r, l)


def _lane_sum_to_row(x, ones):
    hi, mid, lo = _split3(x)
    return (_dot_nt(ones, hi) + _dot_nt(ones, mid) + _dot_nt(ones, lo))[0:1]


def _fox_decode_kernel(pt_ref, qb_ref, knb_ref, vnb_ref, lfn_ref, scan_ref, ones_ref, *rest, n_pages_step):
    g = n_pages_step
    k_refs = rest[0:g]
    v_refs = rest[g:2 * g]
    lf_refs = rest[2 * g:3 * g]
    o_ref = rest[3 * g]
    m_sc, l_sc, r_sc, acc_sc = rest[3 * g + 1:]
    n = pl.program_id(1)
    _, n_heads, head_dim, page = qb_ref.shape

    @pl.when(n == 0)
    def _():
        first = lax.broadcasted_iota(jnp.int32, (SUBLANES, page), 1) == 0
        first_d = lax.broadcasted_iota(jnp.int32, (head_dim, page), 1) == 0
        for h in range(n_heads):
            s_new = _sublane_allsum(_tile_sum(qb_ref[0, h] * knb_ref[0, h]))
            m_sc[h] = jnp.where(first, s_new, NEG_BIG)
            l_sc[h] = jnp.where(first, 1.0, 0.0)
            acc_sc[h] = jnp.where(first_d, vnb_ref[0, h], 0.0)
        r_sc[...] = lfn_ref[0]

    lf_all = jnp.concatenate([lf_refs[t][0] for t in range(g)], axis=0)
    scans = _dot_exact_lhs(lf_all, scan_ref[...])
    r = r_sc[...]
    biases = []
    for t in range(g):
        rows = slice(t * n_heads, (t + 1) * n_heads)
        biases.append(r + scans[rows, 0:page])
        r = r + scans[rows, page:2 * page]
    r_sc[...] = r

    for h in range(n_heads):
        qb = qb_ref[0, h]
        ss = []
        for t in range(g):
            s = _sublane_allsum(_tile_sum(k_refs[t][0, h] * qb))
            ss.append(s + jnp.broadcast_to(biases[t][h:h + 1], s.shape))
        m_old = m_sc[h]
        m_new = functools.reduce(jnp.maximum, ss, m_old)
        alpha = jnp.exp(m_old - m_new)
        l = l_sc[h] * alpha
        acc = _scale_cols(acc_sc[h], alpha)
        for t in range(g):
            p = jnp.exp(ss[t] - m_new)
            l = l + p
            acc = acc + _scale_cols(v_refs[t][0, h], p)
        m_sc[h] = m_new
        l_sc[h] = l
        acc_sc[h] = acc

    @pl.when(n == pl.num_programs(1) - 1)
    def _():
        for h in range(n_heads):
            m = m_sc[h]
            w = jnp.exp(m - jnp.max(m, axis=-1, keepdims=True))
            l_tot = jnp.sum(l_sc[h] * w, axis=-1, keepdims=True)
            o = _lane_sum_to_row(_scale_cols(acc_sc[h], w), ones_ref[...])
            o_ref[0, h:h + 1, :] = o / l_tot[0:1]


def _fox_decode(page_table, qb, knb, vnb, lfn, cache_kt, cache_vt, cache_lft):
    Bd, n_pages = page_table.shape
    g = PAGES_PER_STEP
    _, n_heads, head_dim, page = cache_kt.shape
    pt_flat = page_table.reshape(-1)
    scan = jnp.asarray(np.concatenate([np.tril(np.ones((page, page), np.float32), -1),
                                       np.ones((page, page), np.float32)], axis=1), BF16)
    ones = jnp.ones((2 * SUBLANES, page), BF16)

    def page_spec(shape, t):
        def index_map(b, n, pt):
            return (pt[b * n_pages + n_pages - 1 - (n * g + t)],) + (0,) * (len(shape) - 1)
        return pl.BlockSpec(shape, index_map)

    per_b = lambda a: pl.BlockSpec((1,) + a.shape[1:], lambda b, n, pt: (b,) + (0,) * (a.ndim - 1))
    const = lambda a: pl.BlockSpec(a.shape, lambda b, n, pt: (0,) * a.ndim)
    in_specs = ([per_b(qb), per_b(knb), per_b(vnb), per_b(lfn), const(scan), const(ones)]
                + [page_spec((1, n_heads, head_dim, page), t) for t in range(g)]
                + [page_spec((1, n_heads, head_dim, page), t) for t in range(g)]
                + [page_spec((1, n_heads, page), t) for t in range(g)])
    grid_spec = pltpu.PrefetchScalarGridSpec(
        num_scalar_prefetch=1,
        grid=(Bd, n_pages // g),
        in_specs=in_specs,
        out_specs=pl.BlockSpec((1, n_heads, head_dim), lambda b, n, pt: (b, 0, 0)),
        scratch_shapes=[pltpu.VMEM((n_heads, SUBLANES, page), F32), pltpu.VMEM((n_heads, SUBLANES, page), F32),
                        pltpu.VMEM((n_heads, page), F32), pltpu.VMEM((n_heads, head_dim, page), F32)],
    )
    return pl.pallas_call(
        functools.partial(_fox_decode_kernel, n_pages_step=g),
        grid_spec=grid_spec,
        out_shape=jax.ShapeDtypeStruct((Bd, n_heads, head_dim), F32),
        compiler_params=pltpu.CompilerParams(dimension_semantics=("arbitrary", "arbitrary"),
                                             vmem_limit_bytes=VMEM_LIMIT),
    )(pt_flat, qb, knb, vnb, lfn, scan, ones, *([cache_kt] * g), *([cache_vt] * g), *([cache_lft] * g))


def _mem_attn_sample_kernel(qb_ref, mk_ref, mv_ref, ones_ref, o_ref):
    n_heads = qb_ref.shape[1]
    for h in range(n_heads):
        s = _sublane_allsum(_tile_sum(mk_ref[0, h] * qb_ref[0, h]))
        p = jnp.exp(s - jnp.max(s, axis=-1, keepdims=True))
        l = jnp.sum(p, axis=-1, keepdims=True)
        o = _lane_sum_to_row(_scale_cols(mv_ref[0, h], p), ones_ref[...])
        o_ref[0, h:h + 1, :] = o / l[0:1]


def _mem_attn_sample(qb, mkt, mvt):
    Bd, n_heads, head_dim, m_len = mkt.shape
    ones = jnp.ones((2 * SUBLANES, m_len), BF16)
    per_b = pl.BlockSpec((1, n_heads, head_dim, m_len), lambda b: (b, 0, 0, 0))
    return pl.pallas_call(
        _mem_attn_sample_kernel,
        grid=(Bd,),
        in_specs=[per_b, per_b, per_b, pl.BlockSpec(ones.shape, lambda b: (0, 0))],
        out_specs=pl.BlockSpec((1, n_heads, head_dim), lambda b: (b, 0, 0)),
        out_shape=jax.ShapeDtypeStruct((Bd, n_heads, head_dim), F32),
        compiler_params=pltpu.CompilerParams(dimension_semantics=("arbitrary",)),
    )(qb, mkt, mvt, ones)


def _lane_broadcast(x, n):
    return jnp.broadcast_to(x[..., None], x.shape + (n,))


def kernel(x_prompt, x_sample, mem_prompt, cache_k, cache_v, cache_logf, state_conv, cache_mem_k, cache_mem_v,
           page_table, norm1_g, w_in, b_f, conv_w, mem_norm_g, w_mem_kv, out_norm_g, w_out, norm2_g,
           w_gate, w_up, w_down, final_norm_g):
    B, S, D = x_prompt.shape
    Bd, Td, _ = x_sample.shape
    depth, n_pool, page, n_heads, head_dim = cache_k.shape
    mem_heads = cache_mem_k.shape[3]
    mem_len = cache_mem_k.shape[2]
    conv_k, cw_width = conv_w.shape[1], conv_w.shape[2]
    fox_width = n_heads * head_dim
    mem_width = mem_heads * head_dim
    assert depth == 1, "one layer: the fused tail applies the final norm"
    assert Td == 1 and conv_k == 3 and head_dim * 2 == LANES and n_heads % 2 == 0 and mem_heads % 2 == 0
    assert S % ROW_TILE == 0 and page_table.shape[1] % PAGES_PER_STEP == 0 and 3 * n_heads < LANES
    assert page % LANES == 0 and mem_len % LANES == 0

    selq, selk = _sel_matrices(n_heads, head_dim)
    row2 = lambda a: a.reshape(1, -1)
    xs = x_sample.reshape(Bd, D)

    offs = np.cumsum([0, cw_width, cw_width, cw_width, fox_width, fox_width, fox_width, n_heads, mem_width])
    wl = w_in[0]
    w_f = jnp.pad(wl[:, offs[6]:offs[7]], ((0, 0), (0, LANES - n_heads)))
    w_cat = jnp.concatenate([wl[:, :offs[6]], wl[:, offs[7]:offs[8]], w_f], axis=1).astype(BF16)
    bf_pad = jnp.pad(b_f[0], (0, LANES - n_heads)).reshape(1, LANES)
    g1 = row2(norm1_g[0])
    post_args = (row2(out_norm_g[0]), w_out[0].astype(BF16), row2(norm2_g[0]), w_gate[0].astype(BF16),
                 w_up[0].astype(BF16), w_down[0].astype(BF16), row2(final_norm_g))

    yconv, qaug, kaug, k_p, v_p, vb, logf_p, qm, utail = _inproj_prompt(
        x_prompt, g1, w_cat, bf_pad, conv_w[0], selq, selk, n_heads=n_heads, head_dim=head_dim, mem_width=mem_width)
    yfox = _fox_prompt(qaug, kaug, vb)
    mk, mv = _mem_kv(mem_prompt, row2(mem_norm_g[0]), w_mem_kv[0].astype(BF16))
    ymem = _mem_attn_prompt(qm, mk, mv)
    y_prompt = _post(x_prompt.reshape(B * S, D), yconv.reshape(B * S, cw_width), yfox.reshape(B * S, fox_width),
                     ymem.reshape(B * S, mem_width), *post_args, tm=ROW_TILE)

    yconv_s, u_s, q_s, k_s, v_s, logf_s, qm_s = _inproj_sample(
        xs, g1, w_cat, bf_pad, conv_w[0], state_conv[0, :, 0], state_conv[0, :, 1],
        n_heads=n_heads, head_dim=head_dim, mem_width=mem_width)
    per_head = lambda a, nh: a.reshape(Bd, nh, head_dim)
    yfox_s = _fox_decode(page_table,
                         _lane_broadcast(per_head(q_s, n_heads), page), _lane_broadcast(per_head(k_s, n_heads), page),
                         _lane_broadcast(per_head(v_s, n_heads), page), _lane_broadcast(logf_s[:, :n_heads], page),
                         jnp.transpose(cache_k[0], (0, 2, 3, 1)), jnp.transpose(cache_v[0], (0, 2, 3, 1)),
                         jnp.transpose(cache_logf[0], (0, 2, 1)))
    ymem_s = _mem_attn_sample(_lane_broadcast(per_head(qm_s, mem_heads), mem_len),
                              jnp.transpose(cache_mem_k[0], (0, 2, 3, 1)), jnp.transpose(cache_mem_v[0], (0, 2, 3, 1)))
    y_sample = _post(xs, yconv_s, yfox_s.reshape(Bd, fox_width), ymem_s.reshape(Bd, mem_width), *post_args, tm=Bd)

    return (y_prompt.reshape(B, S, D), y_sample.reshape(Bd, 1, D),
            k_p.reshape(1, B, S, n_heads, head_dim), v_p.reshape(1, B, S, n_heads, head_dim), logf_p[None],
            utail[None, :, SUBLANES - (conv_k - 1):],
            mk.reshape(1, B, mem_len, mem_heads, head_dim), mv.reshape(1, B, mem_len, mem_heads, head_dim),
            k_s.reshape(1, Bd, 1, n_heads, head_dim), v_s.reshape(1, Bd, 1, n_heads, head_dim),
            logf_s[None, :, None, :n_heads], jnp.stack([state_conv[0, :, 1], u_s], axis=1)[None])
```

```python
import functools

import numpy as np
import jax
import jax.numpy as jnp
from jax import lax
from jax.experimental import pallas as pl
from jax.experimental.pallas import tpu as pltpu

F32 = jnp.float32
BF16 = jnp.bfloat16
EPS = 1e-6
LOG2E = 1.4426950408889634

LANES = 128
SUBLANES = 8
ROW_TILE = 512
MXU_WIDTH = 256
PAGES_PER_STEP = 16
VMEM_LIMIT = 56 * 1024 * 1024


def _dot(a, b):
    return jnp.dot(a, b, preferred_element_type=F32)


def _dot_nt(a, b):
    return lax.dot_general(a, b, (((1,), (1,)), ((), ())), preferred_element_type=F32)


def _normalize(x):
    return x * lax.rsqrt(jnp.mean(x * x, axis=-1, keepdims=True) + EPS)


def _split3(x):
    hi = x.astype(BF16)
    r = x - hi.astype(F32)
    mid = r.astype(BF16)
    lo = (r - mid.astype(F32)).astype(BF16)
    return hi, mid, lo


def _dot_exact_lhs(x, m):
    hi, mid, lo = _split3(x)
    return _dot(hi, m) + _dot(mid, m) + _dot(lo, m)


def _log_sigmoid(x):
    return jnp.minimum(x, 0.0) - jnp.log1p(jnp.exp(-jnp.abs(x)))


def _inproj_prompt_kernel(x_ref, g_ref, w_ref, wt_ref, bf_ref, cw_ref, selqt_ref, selk_ref,
                          yconv_ref, qaugt_ref, kaug_ref, k_ref, vt_ref, vtb_ref, logf_ref, qm_ref, utail_ref,
                          ucarry, ccarry, *, cw_width, fox_width, mem_width, n_heads, scale):
    t = pl.program_id(1)
    tm = x_ref.shape[1]
    head_dim = fox_width // n_heads

    @pl.when(t == 0)
    def _():
        ucarry[...] = jnp.zeros_like(ucarry)
        ccarry[...] = jnp.zeros_like(ccarry)

    x = x_ref[0]
    xn = (_normalize(x) * g_ref[...]).astype(BF16)

    o_k = 3 * cw_width
    o_m = o_k + fox_width
    o_f = o_m + mem_width

    pc = _dot(xn, w_ref[:, 0:o_k])
    b_c = pc[:, 0:cw_width]
    u = pc[:, cw_width:2 * cw_width] * pc[:, 2 * cw_width:3 * cw_width]
    prev = ucarry[...]
    row8 = lax.broadcasted_iota(jnp.int32, (SUBLANES, cw_width), 0)
    r1 = pltpu.roll(u, 1, 0)
    r2 = pltpu.roll(u, 2, 0)
    h1 = jnp.where(row8 < 1, pltpu.roll(prev, 1, 0), r1[0:SUBLANES])
    h2 = jnp.where(row8 < 2, pltpu.roll(prev, 2, 0), r2[0:SUBLANES])
    u1 = jnp.concatenate([h1, r1[SUBLANES:]], axis=0)
    u2 = jnp.concatenate([h2, r2[SUBLANES:]], axis=0)
    cw = cw_ref[...]
    yconv_ref[0] = b_c * (cw[0:1] * u2 + cw[1:2] * u1 + cw[2:3] * u)
    tail = u[tm - SUBLANES:]
    ucarry[...] = tail
    utail_ref[0] = tail

    pf = _dot(xn, w_ref[:, o_m:o_f + LANES])
    qm_ref[0] = (pf[:, 0:mem_width] * scale).astype(BF16)
    logf = _log_sigmoid(pf[:, mem_width:mem_width + LANES] + bf_ref[...])
    logf_ref[0] = logf[:, 0:n_heads]

    lane = lax.broadcasted_iota(jnp.int32, (tm, LANES), 1)
    row = lax.broadcasted_iota(jnp.int32, (tm, LANES), 0)
    c = jnp.where(lane < n_heads, logf, 0.0)
    sh = 1
    while sh < tm:
        c = c + jnp.where(row >= sh, pltpu.roll(c, sh, 0), 0.0)
        sh *= 2
    c = c + ccarry[0:1]
    ccarry[...] = jnp.broadcast_to(c[tm - 1:tm], ccarry.shape)

    cs = c * LOG2E
    hi = cs.astype(BF16).astype(F32)
    r = cs - hi
    mid = r.astype(BF16).astype(F32)
    lo = r - mid
    parts = (hi + pltpu.roll(mid, n_heads, 1) + pltpu.roll(lo, 2 * n_heads, 1)
             + jnp.where(lane == 3 * n_heads, 1.0, 0.0)).astype(BF16)

    ek = _dot(parts, selk_ref[...])
    pk = _dot(xn, w_ref[:, o_k:o_m])
    k_ref[0] = pk
    half = lane < (LANES // 2)
    for hp in range(n_heads // 2):
        pair = pk[:, hp * LANES:(hp + 1) * LANES]
        rot = pltpu.roll(pair, LANES // 2, 1)
        e0 = ek[:, (2 * hp) * LANES:(2 * hp + 1) * LANES]
        e1 = ek[:, (2 * hp + 1) * LANES:(2 * hp + 2) * LANES]
        kaug_ref[0, 2 * hp] = jnp.where(half, pair, e0).astype(BF16)
        kaug_ref[0, 2 * hp + 1] = jnp.where(half, rot, e1).astype(BF16)

    qt = _dot_nt(wt_ref[0:fox_width, :], xn) * (scale * LOG2E)
    vt = _dot_nt(wt_ref[fox_width:2 * fox_width, :], xn)
    vt_ref[0] = vt
    ext = _dot_nt(selqt_ref[...], parts)
    pad = jnp.zeros((LANES - head_dim - SUBLANES, tm), F32)
    ones_rows = jnp.where(lax.broadcasted_iota(jnp.int32, (2 * SUBLANES, tm), 0) == 0, 1.0, 0.0)
    for h in range(n_heads):
        vtb_ref[0, h, 0] = jnp.concatenate([vt[h * head_dim:(h + 1) * head_dim], ones_rows], axis=0).astype(BF16)
        qaugt_ref[0, h] = jnp.concatenate(
            [qt[h * head_dim:(h + 1) * head_dim], ext[h * SUBLANES:(h + 1) * SUBLANES], pad], axis=0).astype(BF16)


def _sel_matrices(n_heads, head_dim):
    selqt = np.zeros((n_heads * SUBLANES, LANES), np.float32)
    selk = np.zeros((LANES, n_heads * LANES), np.float32)
    for h in range(n_heads):
        base = h * LANES + head_dim
        for part in range(3):
            selqt[h * SUBLANES + part, part * n_heads + h] = 1.0
            selqt[h * SUBLANES + 3 + part, 3 * n_heads] = 1.0
            selk[3 * n_heads, base + part] = 1.0
            selk[part * n_heads + h, base + 3 + part] = -1.0
    return jnp.asarray(selqt, BF16), jnp.asarray(selk, BF16)


def _inproj_prompt(x, g, w, wt, bf, cw, selqt, selk, *, n_heads, head_dim, mem_width):
    B, S, D = x.shape
    cw_width = cw.shape[1]
    fox_width = n_heads * head_dim
    tm = ROW_TILE
    nt = S // tm
    kern = functools.partial(_inproj_prompt_kernel, cw_width=cw_width, fox_width=fox_width,
                             mem_width=mem_width, n_heads=n_heads, scale=head_dim ** -0.5)
    const = lambda a: pl.BlockSpec(a.shape, lambda b, t: (0,) * a.ndim)
    row_spec = lambda width: pl.BlockSpec((1, tm, width), lambda b, t: (b, t, 0))
    return pl.pallas_call(
        kern,
        grid=(B, nt),
        in_specs=[row_spec(D), const(g), const(w), const(wt), const(bf), const(cw), const(selqt), const(selk)],
        out_specs=[row_spec(cw_width),
                   pl.BlockSpec((1, n_heads, LANES, tm), lambda b, t: (b, 0, 0, t)),
                   pl.BlockSpec((1, n_heads, tm, LANES), lambda b, t: (b, 0, t, 0)),
                   row_spec(fox_width),
                   pl.BlockSpec((1, fox_width, tm), lambda b, t: (b, 0, t)),
                   pl.BlockSpec((1, n_heads, 1, head_dim + 2 * SUBLANES, tm), lambda b, t: (b, 0, t, 0, 0)),
                   row_spec(n_heads), row_spec(mem_width),
                   pl.BlockSpec((1, SUBLANES, cw_width), lambda b, t: (b, 0, 0))],
        out_shape=[jax.ShapeDtypeStruct((B, S, cw_width), F32),
                   jax.ShapeDtypeStruct((B, n_heads, LANES, S), BF16),
                   jax.ShapeDtypeStruct((B, n_heads, S, LANES), BF16),
                   jax.ShapeDtypeStruct((B, S, fox_width), F32),
                   jax.ShapeDtypeStruct((B, fox_width, S), F32),
                   jax.ShapeDtypeStruct((B, n_heads, nt, head_dim + 2 * SUBLANES, tm), BF16),
                   jax.ShapeDtypeStruct((B, S, n_heads), F32),
                   jax.ShapeDtypeStruct((B, S, mem_width), BF16),
                   jax.ShapeDtypeStruct((B, SUBLANES, cw_width), F32)],
        scratch_shapes=[pltpu.VMEM((SUBLANES, cw_width), F32), pltpu.VMEM((SUBLANES, LANES), F32)],
        compiler_params=pltpu.CompilerParams(dimension_semantics=("arbitrary", "arbitrary"),
                                             vmem_limit_bytes=VMEM_LIMIT),
    )(x, g, w, wt, bf, cw, selqt, selk)


def _fox_prompt_kernel(qt_ref, k_ref, vt_ref, o_ref, sta_sc, stb_sc, m_sc, acc_sc, *, head_dim):
    i = pl.program_id(2)
    tq = qt_ref.shape[3]
    tk = tq
    m_sc[...] = jnp.full_like(m_sc, -jnp.inf)
    acc_sc[...] = jnp.zeros_like(acc_sc)

    def scores(j, st_ref):
        off = pl.multiple_of(j * tk, tk)
        for hh in range(2):
            st_ref[hh] = _dot(k_ref[0, hh, pl.ds(off, tk), :], qt_ref[0, hh])

    def update(j, st_ref, masked):
        for hh in range(2):
            st = st_ref[hh]
            if masked:
                kpos = lax.broadcasted_iota(jnp.int32, st.shape, 0)
                qpos = lax.broadcasted_iota(jnp.int32, st.shape, 1)
                st = jnp.where(kpos <= qpos, st, -jnp.inf)
            m_old = m_sc[hh]
            m_new = jnp.maximum(m_old, jnp.max(st, axis=0, keepdims=True))
            p = jnp.exp2(st - m_new)
            acc_sc[hh] = jnp.exp2(m_old - m_new) * acc_sc[hh] + _dot(vt_ref[0, hh, j], p.astype(BF16))
            m_sc[hh] = m_new

    scores(0, sta_sc)

    def body(t, carry):
        scores(2 * t + 1, stb_sc)
        update(2 * t, sta_sc, False)
        scores(2 * t + 2, sta_sc)
        update(2 * t + 1, stb_sc, False)
        return carry

    lax.fori_loop(0, i // 2, body, 0)

    @pl.when(i % 2 == 0)
    def _():
        update(i, sta_sc, True)

    @pl.when(i % 2 == 1)
    def _():
        scores(i, stb_sc)
        update(i - 1, sta_sc, False)
        update(i, stb_sc, True)

    ot = jnp.concatenate([acc_sc[hh, 0:head_dim] / acc_sc[hh, head_dim:head_dim + 1] for hh in range(2)], axis=0)
    o_ref[0] = ot.T


def _fox_prompt(qaugt, kaug, vtb, *, head_dim):
    B, H, _, S = qaugt.shape
    _, _, nt, v_rows, tk = vtb.shape
    tq = ROW_TILE
    assert tk == tq
    return pl.pallas_call(
        functools.partial(_fox_prompt_kernel, head_dim=head_dim),
        grid=(B, H // 2, S // tq),
        in_specs=[pl.BlockSpec((1, 2, LANES, tq), lambda b, hp, i: (b, hp, 0, i)),
                  pl.BlockSpec((1, 2, S, LANES), lambda b, hp, i: (b, hp, 0, 0)),
                  pl.BlockSpec((1, 2, nt, v_rows, tk), lambda b, hp, i: (b, hp, 0, 0, 0))],
        out_specs=pl.BlockSpec((1, tq, 2 * head_dim), lambda b, hp, i: (b, i, hp)),
        out_shape=jax.ShapeDtypeStruct((B, S, H * head_dim), F32),
        scratch_shapes=[pltpu.VMEM((2, tk, tq), F32), pltpu.VMEM((2, tk, tq), F32), pltpu.VMEM((2, 1, tq), F32),
                        pltpu.VMEM((2, v_rows, tq), F32)],
        compiler_params=pltpu.CompilerParams(dimension_semantics=("arbitrary", "arbitrary", "arbitrary"),
                                             vmem_limit_bytes=VMEM_LIMIT),
    )(qaugt, kaug, vtb)


def _mem_kv_kernel(mem_ref, g_ref, w_ref, mk_ref, mv_ref):
    h = (_normalize(mem_ref[0]) * g_ref[...]).astype(BF16)
    kv = _dot(h, w_ref[...])
    width = mk_ref.shape[2]
    mk_ref[0] = kv[:, 0:width]
    mv_ref[0] = kv[:, width:2 * width]


def _mem_kv(mem, g, w):
    B, M, D = mem.shape
    width = w.shape[1] // 2
    out = jax.ShapeDtypeStruct((B, M, width), F32)
    return pl.pallas_call(
        _mem_kv_kernel,
        grid=(B,),
        in_specs=[pl.BlockSpec((1, M, D), lambda b: (b, 0, 0)),
                  pl.BlockSpec(g.shape, lambda b: (0, 0)),
                  pl.BlockSpec(w.shape, lambda b: (0, 0))],
        out_specs=[pl.BlockSpec((1, M, width), lambda b: (b, 0, 0))] * 2,
        out_shape=[out, out],
        compiler_params=pltpu.CompilerParams(dimension_semantics=("arbitrary",)),
    )(mem, g, w)


def _mem_attn_prompt_kernel(q_ref, mk_ref, mv_ref, o_ref):
    tm = q_ref.shape[1]
    n_pairs = q_ref.shape[2] // LANES
    lane_k = lax.broadcasted_iota(jnp.int32, (mk_ref.shape[1], LANES), 1)
    lane_o = lax.broadcasted_iota(jnp.int32, (tm, LANES), 1)
    for hp in range(n_pairs):
        sl = slice(hp * LANES, (hp + 1) * LANES)
        q = q_ref[0, :, sl]
        mk = mk_ref[0, :, sl]
        mv = mv_ref[0, :, sl].astype(BF16)
        outs = []
        for hh in range(2):
            in_head = (lane_k < LANES // 2) if hh == 0 else (lane_k >= LANES // 2)
            s = _dot_nt(q, jnp.where(in_head, mk, 0.0).astype(BF16))
            p = jnp.exp(s - jnp.max(s, axis=-1, keepdims=True))
            l = jnp.sum(p, axis=-1, keepdims=True)
            outs.append(_dot(p.astype(BF16), mv) / l)
        o_ref[0, :, sl] = jnp.where(lane_o < LANES // 2, outs[0], outs[1])


def _mem_attn_prompt(qm, mk, mv):
    B, S, W = qm.shape
    M = mk.shape[1]
    tm = ROW_TILE
    return pl.pallas_call(
        _mem_attn_prompt_kernel,
        grid=(B, S // tm),
        in_specs=[pl.BlockSpec((1, tm, W), lambda b, t: (b, t, 0)),
                  pl.BlockSpec((1, M, W), lambda b, t: (b, 0, 0)),
                  pl.BlockSpec((1, M, W), lambda b, t: (b, 0, 0))],
        out_specs=pl.BlockSpec((1, tm, W), lambda b, t: (b, t, 0)),
        out_shape=jax.ShapeDtypeStruct((B, S, W), F32),
        compiler_params=pltpu.CompilerParams(dimension_semantics=("arbitrary", "arbitrary")),
    )(qm, mk, mv)


def _post_kernel(x_ref, yc_ref, yf_ref, ym_ref, og_ref, wo_ref, n2_ref, wg_ref, wu_ref, wd_ref, fg_ref, o_ref,
                 *, ff_split):
    og = og_ref[...]
    wc = yc_ref.shape[1]
    wf = yf_ref.shape[1]
    wm = ym_ref.shape[1]
    a = (_normalize(yc_ref[...]) * og[:, 0:wc]).astype(BF16)
    b = (_normalize(yf_ref[...]) * og[:, wc:wc + wf]).astype(BF16)
    c = (_normalize(ym_ref[...]) * og[:, wc + wf:wc + wf + wm]).astype(BF16)
    mix = (_dot(a, wo_ref[0:wc, :]) + _dot(b, wo_ref[wc:wc + wf, :])
           + _dot(c, wo_ref[wc + wf:wc + wf + wm, :]))
    x1 = x_ref[...] + mix
    h = (_normalize(x1) * n2_ref[...]).astype(BF16)
    d_ff = wg_ref.shape[1]
    ffn = jnp.zeros_like(x1)
    for sl in (slice(0, ff_split), slice(ff_split, d_ff)):
        gate = _dot(h, wg_ref[:, sl])
        up = _dot(h, wu_ref[:, sl])
        act = (gate * jax.nn.sigmoid(gate) * up).astype(BF16)
        ffn = ffn + _dot(act, wd_ref[sl, :])
    x2 = x1 + ffn
    o_ref[...] = _normalize(x2) * fg_ref[...]


def _post(x, yc, yf, ym, og, wo, n2, wg, wu, wd, fg, *, tm):
    N, D = x.shape
    d_ff = wg.shape[1]
    ff_split = pl.cdiv(d_ff // 2, MXU_WIDTH) * MXU_WIDTH
    const = lambda a: pl.BlockSpec(a.shape, lambda t: (0, 0), pipeline_mode=pl.Buffered(1))
    rows = lambda a: pl.BlockSpec((tm, a.shape[1]), lambda t: (t, 0))
    return pl.pallas_call(
        functools.partial(_post_kernel, ff_split=ff_split),
        grid=(N // tm,),
        in_specs=[rows(x), rows(yc), rows(yf), rows(ym), const(og), const(wo), const(n2), const(wg), const(wu),
                  const(wd), const(fg)],
        out_specs=pl.BlockSpec((tm, D), lambda t: (t, 0)),
        out_shape=jax.ShapeDtypeStruct((N, D), F32),
        compiler_params=pltpu.CompilerParams(dimension_semantics=("arbitrary",), vmem_limit_bytes=VMEM_LIMIT),
    )(x, yc, yf, ym, og, wo, n2, wg, wu, wd, fg)


def _inproj_sample_kernel(x_ref, g_ref, w_ref, bf_ref, cw_ref, h0_ref, h1_ref,
                          yconv_ref, u_ref, q_ref, k_ref, v_ref, logf_ref, qm_ref,
                          *, cw_width, fox_width, mem_width, scale):
    xn = (_normalize(x_ref[...]) * g_ref[...]).astype(BF16)
    p = _dot(xn, w_ref[...])
    o_q = 3 * cw_width
    o_k = o_q + fox_width
    o_v = o_k + fox_width
    o_m = o_v + fox_width
    o_f = o_m + mem_width
    b_c = p[:, 0:cw_width]
    u = p[:, cw_width:2 * cw_width] * p[:, 2 * cw_width:3 * cw_width]
    cw = cw_ref[...]
    yconv_ref[...] = b_c * (cw[0:1] * h0_ref[...] + cw[1:2] * h1_ref[...] + cw[2:3] * u)
    u_ref[...] = u
    q_ref[...] = p[:, o_q:o_k] * scale
    k_ref[...] = p[:, o_k:o_v]
    v_ref[...] = p[:, o_v:o_m]
    qm_ref[...] = p[:, o_m:o_f] * scale
    logf_ref[...] = _log_sigmoid(p[:, o_f:o_f + LANES] + bf_ref[...])


def _inproj_sample(x, g, w, bf, cw, h0, h1, *, n_heads, head_dim, mem_width):
    N, D = x.shape
    cw_width = cw.shape[1]
    fox_width = n_heads * head_dim
    kern = functools.partial(_inproj_sample_kernel, cw_width=cw_width, fox_width=fox_width,
                             mem_width=mem_width, scale=head_dim ** -0.5)
    sds = lambda width: jax.ShapeDtypeStruct((N, width), F32)
    return pl.pallas_call(
        kern,
        out_shape=[sds(cw_width), sds(cw_width), sds(fox_width), sds(fox_width), sds(fox_width), sds(LANES),
                   sds(mem_width)],
        compiler_params=pltpu.CompilerParams(vmem_limit_bytes=VMEM_LIMIT),
    )(x, g, w, bf, cw, h0, h1)


NEG_BIG = -1e30


def _tile_sum(x):
    return jnp.sum(x.reshape(x.shape[0] // SUBLANES, SUBLANES, x.shape[1]), axis=0)


def _sublane_allsum(x):
    x = x + pltpu.roll(x, 4, 0)
    x = x + pltpu.roll(x, 2, 0)
    return x + pltpu.roll(x, 1, 0)


def _scale_cols(x, w):
    r, l = x.shape
    return (x.reshape(r // SUBLANES, SUBLANES, l) * w[None]).reshape(r, l)


def _lane_sum_to_row(x, ones):
    hi, mid, lo = _split3(x)
    return (_dot_nt(ones, hi) + _dot_nt(ones, mid) + _dot_nt(ones, lo))[0:1]


def _fox_decode_kernel(pt_ref, qb_ref, knb_ref, vnb_ref, lfn_ref, scan_ref, ones_ref, *rest, n_pages_step):
    g = n_pages_step
    k_refs = rest[0:g]
    v_refs = rest[g:2 * g]
    lf_refs = rest[2 * g:3 * g]
    o_ref = rest[3 * g]
    m_sc, l_sc, r_sc, acc_sc = rest[3 * g + 1:]
    n = pl.program_id(1)
    _, n_heads, head_dim, page = qb_ref.shape

    @pl.when(n == 0)
    def _():
        first = lax.broadcasted_iota(jnp.int32, (SUBLANES, page), 1) == 0
        first_d = lax.broadcasted_iota(jnp.int32, (head_dim, page), 1) == 0
        for h in range(n_heads):
            s_new = _sublane_allsum(_tile_sum(qb_ref[0, h] * knb_ref[0, h]))
            m_sc[h] = jnp.where(first, s_new, NEG_BIG)
            l_sc[h] = jnp.where(first, 1.0, 0.0)
            acc_sc[h] = jnp.where(first_d, vnb_ref[0, h], 0.0)
        r_sc[...] = lfn_ref[0]

    lf_all = jnp.concatenate([lf_refs[t][0] for t in range(g)], axis=0)
    scans = _dot_exact_lhs(lf_all, scan_ref[...])
    r = r_sc[...]
    biases = []
    for t in range(g):
        rows = slice(t * n_heads, (t + 1) * n_heads)
        biases.append(r + scans[rows, 0:page])
        r = r + scans[rows, page:2 * page]
    r_sc[...] = r

    for h in range(n_heads):
        qb = qb_ref[0, h]
        ss = []
        for t in range(g):
            s = _sublane_allsum(_tile_sum(k_refs[t][0, h] * qb))
            ss.append(s + jnp.broadcast_to(biases[t][h:h + 1], s.shape))
        m_old = m_sc[h]
        m_new = functools.reduce(jnp.maximum, ss, m_old)
        alpha = jnp.exp(m_old - m_new)
        l = l_sc[h] * alpha
        acc = _scale_cols(acc_sc[h], alpha)
        for t in range(g):
            p = jnp.exp(ss[t] - m_new)
            l = l + p
            acc = acc + _scale_cols(v_refs[t][0, h], p)
        m_sc[h] = m_new
        l_sc[h] = l
        acc_sc[h] = acc

    @pl.when(n == pl.num_programs(1) - 1)
    def _():
        for h in range(n_heads):
            m = m_sc[h]
            w = jnp.exp(m - jnp.max(m, axis=-1, keepdims=True))
            l_tot = jnp.sum(l_sc[h] * w, axis=-1, keepdims=True)
            o = _lane_sum_to_row(_scale_cols(acc_sc[h], w), ones_ref[...])
            o_ref[0, h:h + 1, :] = o / l_tot[0:1]


def _fox_decode(page_table, qb, knb, vnb, lfn, cache_kt, cache_vt, cache_lft):
    Bd, n_pages = page_table.shape
    g = PAGES_PER_STEP
    _, n_heads, head_dim, page = cache_kt.shape
    pt_flat = page_table.reshape(-1)
    scan = jnp.asarray(np.concatenate([np.tril(np.ones((page, page), np.float32), -1),
                                       np.ones((page, page), np.float32)], axis=1), BF16)
    ones = jnp.ones((2 * SUBLANES, page), BF16)

    def page_spec(shape, t):
        def index_map(b, n, pt):
            return (pt[b * n_pages + n_pages - 1 - (n * g + t)],) + (0,) * (len(shape) - 1)
        return pl.BlockSpec(shape, index_map)

    per_b = lambda a: pl.BlockSpec((1,) + a.shape[1:], lambda b, n, pt: (b,) + (0,) * (a.ndim - 1))
    const = lambda a: pl.BlockSpec(a.shape, lambda b, n, pt: (0,) * a.ndim)
    in_specs = ([per_b(qb), per_b(knb), per_b(vnb), per_b(lfn), const(scan), const(ones)]
                + [page_spec((1, n_heads, head_dim, page), t) for t in range(g)]
                + [page_spec((1, n_heads, head_dim, page), t) for t in range(g)]
                + [page_spec((1, n_heads, page), t) for t in range(g)])
    grid_spec = pltpu.PrefetchScalarGridSpec(
        num_scalar_prefetch=1,
        grid=(Bd, n_pages // g),
        in_specs=in_specs,
        out_specs=pl.BlockSpec((1, n_heads, head_dim), lambda b, n, pt: (b, 0, 0)),
        scratch_shapes=[pltpu.VMEM((n_heads, SUBLANES, page), F32), pltpu.VMEM((n_heads, SUBLANES, page), F32),
                        pltpu.VMEM((n_heads, page), F32), pltpu.VMEM((n_heads, head_dim, page), F32)],
    )
    return pl.pallas_call(
        functools.partial(_fox_decode_kernel, n_pages_step=g),
        grid_spec=grid_spec,
        out_shape=jax.ShapeDtypeStruct((Bd, n_heads, head_dim), F32),
        compiler_params=pltpu.CompilerParams(dimension_semantics=("arbitrary", "arbitrary"),
                                             vmem_limit_bytes=VMEM_LIMIT),
    )(pt_flat, qb, knb, vnb, lfn, scan, ones, *([cache_kt] * g), *([cache_vt] * g), *([cache_lft] * g))


def _mem_attn_sample_kernel(qb_ref, mk_ref, mv_ref, ones_ref, o_ref):
    n_heads = qb_ref.shape[1]
    for h in range(n_heads):
        s = _sublane_allsum(_tile_sum(mk_ref[0, h] * qb_ref[0, h]))
        p = jnp.exp(s - jnp.max(s, axis=-1, keepdims=True))
        l = jnp.sum(p, axis=-1, keepdims=True)
        o = _lane_sum_to_row(_scale_cols(mv_ref[0, h], p), ones_ref[...])
        o_ref[0, h:h + 1, :] = o / l[0:1]


def _mem_attn_sample(qb, mkt, mvt):
    Bd, n_heads, head_dim, m_len = mkt.shape
    ones = jnp.ones((2 * SUBLANES, m_len), BF16)
    per_b = pl.BlockSpec((1, n_heads, head_dim, m_len), lambda b: (b, 0, 0, 0))
    return pl.pallas_call(
        _mem_attn_sample_kernel,
        grid=(Bd,),
        in_specs=[per_b, per_b, per_b, pl.BlockSpec(ones.shape, lambda b: (0, 0))],
        out_specs=pl.BlockSpec((1, n_heads, head_dim), lambda b: (b, 0, 0)),
        out_shape=jax.ShapeDtypeStruct((Bd, n_heads, head_dim), F32),
        compiler_params=pltpu.CompilerParams(dimension_semantics=("arbitrary",)),
    )(qb, mkt, mvt, ones)


def _lane_broadcast(x, n):
    return jnp.broadcast_to(x[..., None], x.shape + (n,))


def kernel(x_prompt, x_sample, mem_prompt, cache_k, cache_v, cache_logf, state_conv, cache_mem_k, cache_mem_v,
           page_table, norm1_g, w_in, b_f, conv_w, mem_norm_g, w_mem_kv, out_norm_g, w_out, norm2_g,
           w_gate, w_up, w_down, final_norm_g):
    B, S, D = x_prompt.shape
    Bd, Td, _ = x_sample.shape
    depth, n_pool, page, n_heads, head_dim = cache_k.shape
    mem_heads = cache_mem_k.shape[3]
    mem_len = cache_mem_k.shape[2]
    conv_k, cw_width = conv_w.shape[1], conv_w.shape[2]
    fox_width = n_heads * head_dim
    mem_width = mem_heads * head_dim
    assert depth == 1, "one layer: the fused tail applies the final norm"
    assert Td == 1 and conv_k == 3 and head_dim * 2 == LANES and n_heads % 2 == 0 and mem_heads % 2 == 0
    assert S % ROW_TILE == 0 and page_table.shape[1] % PAGES_PER_STEP == 0 and 3 * n_heads < LANES
    assert page % LANES == 0 and mem_len % LANES == 0

    selqt, selk = _sel_matrices(n_heads, head_dim)
    row2 = lambda a: a.reshape(1, -1)
    xs = x_sample.reshape(Bd, D)

    offs = np.cumsum([0, cw_width, cw_width, cw_width, fox_width, fox_width, fox_width, n_heads, mem_width])
    wl = w_in[0]
    w_f = jnp.pad(wl[:, offs[6]:offs[7]], ((0, 0), (0, LANES - n_heads)))
    w_cat = jnp.concatenate([wl[:, :offs[6]], wl[:, offs[7]:offs[8]], w_f], axis=1).astype(BF16)
    w_nat = jnp.concatenate([wl[:, :offs[3]], wl[:, offs[4]:offs[5]], wl[:, offs[7]:offs[8]], w_f],
                            axis=1).astype(BF16)
    w_t = jnp.concatenate([wl[:, offs[3]:offs[4]], wl[:, offs[5]:offs[6]]], axis=1).T.astype(BF16)
    bf_pad = jnp.pad(b_f[0], (0, LANES - n_heads)).reshape(1, LANES)
    g1 = row2(norm1_g[0])
    post_args = (row2(out_norm_g[0]), w_out[0].astype(BF16), row2(norm2_g[0]), w_gate[0].astype(BF16),
                 w_up[0].astype(BF16), w_down[0].astype(BF16), row2(final_norm_g))

    yconv, qaugt, kaug, k_p, vt_p, vtb, logf_p, qm, utail = _inproj_prompt(
        x_prompt, g1, w_nat, w_t, bf_pad, conv_w[0], selqt, selk,
        n_heads=n_heads, head_dim=head_dim, mem_width=mem_width)
    yfox = _fox_prompt(qaugt, kaug, vtb, head_dim=head_dim)
    v_p = jnp.transpose(vt_p.reshape(B, n_heads, head_dim, S), (0, 3, 1, 2))
    mk, mv = _mem_kv(mem_prompt, row2(mem_norm_g[0]), w_mem_kv[0].astype(BF16))
    ymem = _mem_attn_prompt(qm, mk, mv)
    y_prompt = _post(x_prompt.reshape(B * S, D), yconv.reshape(B * S, cw_width), yfox.reshape(B * S, fox_width),
                     ymem.reshape(B * S, mem_width), *post_args, tm=ROW_TILE)

    yconv_s, u_s, q_s, k_s, v_s, logf_s, qm_s = _inproj_sample(
        xs, g1, w_cat, bf_pad, conv_w[0], state_conv[0, :, 0], state_conv[0, :, 1],
        n_heads=n_heads, head_dim=head_dim, mem_width=mem_width)
    per_head = lambda a, nh: a.reshape(Bd, nh, head_dim)
    yfox_s = _fox_decode(page_table,
                         _lane_broadcast(per_head(q_s, n_heads), page), _lane_broadcast(per_head(k_s, n_heads), page),
                         _lane_broadcast(per_head(v_s, n_heads), page), _lane_broadcast(logf_s[:, :n_heads], page),
                         jnp.transpose(cache_k[0], (0, 2, 3, 1)), jnp.transpose(cache_v[0], (0, 2, 3, 1)),
                         jnp.transpose(cache_logf[0], (0, 2, 1)))
    ymem_s = _mem_attn_sample(_lane_broadcast(per_head(qm_s, mem_heads), mem_len),
                              jnp.transpose(cache_mem_k[0], (0, 2, 3, 1)), jnp.transpose(cache_mem_v[0], (0, 2, 3, 1)))
    y_sample = _post(xs, yconv_s, yfox_s.reshape(Bd, fox_width), ymem_s.reshape(Bd, mem_width), *post_args, tm=Bd)

    return (y_prompt.reshape(B, S, D), y_sample.reshape(Bd, 1, D),
            k_p.reshape(1, B, S, n_heads, head_dim), v_p.reshape(1, B, S, n_heads, head_dim), logf_p[None],
            utail[None, :, SUBLANES - (conv_k - 1):],
            mk.reshape(1, B, mem_len, mem_heads, head_dim), mv.reshape(1, B, mem_len, mem_heads, head_dim),
            k_s.reshape(1, Bd, 1, n_heads, head_dim), v_s.reshape(1, Bd, 1, n_heads, head_dim),
            logf_s[None, :, None, :n_heads], jnp.stack([state_conv[0, :, 1], u_s], axis=1)[None])
```

```python
import functools

import numpy as np
import jax
import jax.numpy as jnp
from jax import lax
from jax.experimental import pallas as pl
from jax.experimental.pallas import tpu as pltpu

F32 = jnp.float32
BF16 = jnp.bfloat16
EPS = 1e-6
LOG2E = 1.4426950408889634

LANES = 128
SUBLANES = 8
ROW_TILE = 512
MXU_WIDTH = 256
PAGES_PER_STEP = 32
VMEM_LIMIT = 56 * 1024 * 1024


def _dot(a, b):
    return jnp.dot(a, b, preferred_element_type=F32)


def _dot_nt(a, b):
    return lax.dot_general(a, b, (((1,), (1,)), ((), ())), preferred_element_type=F32)


def _normalize(x):
    return x * lax.rsqrt(jnp.mean(x * x, axis=-1, keepdims=True) + EPS)


def _split3(x):
    hi = x.astype(BF16)
    r = x - hi.astype(F32)
    mid = r.astype(BF16)
    lo = (r - mid.astype(F32)).astype(BF16)
    return hi, mid, lo


def _dot_exact_lhs(x, m):
    hi, mid, lo = _split3(x)
    return _dot(hi, m) + _dot(mid, m) + _dot(lo, m)


def _log_sigmoid(x):
    return jnp.minimum(x, 0.0) - jnp.log1p(jnp.exp(-jnp.abs(x)))


def _inproj_prompt_kernel(x_ref, g_ref, w_ref, wt_ref, bf_ref, cw_ref, selqt_ref, selk_ref,
                          yconv_ref, qaugt_ref, kaug_ref, k_ref, vt_ref, vtb_ref, logf_ref, qm_ref, utail_ref,
                          ucarry, ccarry, *, cw_width, fox_width, mem_width, n_heads, scale):
    t = pl.program_id(1)
    tm = x_ref.shape[1]
    head_dim = fox_width // n_heads

    @pl.when(t == 0)
    def _():
        ucarry[...] = jnp.zeros_like(ucarry)
        ccarry[...] = jnp.zeros_like(ccarry)

    x = x_ref[0]
    xn = (_normalize(x) * g_ref[...]).astype(BF16)

    o_k = 3 * cw_width
    o_m = o_k + fox_width
    o_f = o_m + mem_width

    pc = _dot(xn, w_ref[:, 0:o_k])
    b_c = pc[:, 0:cw_width]
    u = pc[:, cw_width:2 * cw_width] * pc[:, 2 * cw_width:3 * cw_width]
    prev = ucarry[...]
    row8 = lax.broadcasted_iota(jnp.int32, (SUBLANES, cw_width), 0)
    r1 = pltpu.roll(u, 1, 0)
    r2 = pltpu.roll(u, 2, 0)
    h1 = jnp.where(row8 < 1, pltpu.roll(prev, 1, 0), r1[0:SUBLANES])
    h2 = jnp.where(row8 < 2, pltpu.roll(prev, 2, 0), r2[0:SUBLANES])
    u1 = jnp.concatenate([h1, r1[SUBLANES:]], axis=0)
    u2 = jnp.concatenate([h2, r2[SUBLANES:]], axis=0)
    cw = cw_ref[...]
    yconv_ref[0] = b_c * (cw[0:1] * u2 + cw[1:2] * u1 + cw[2:3] * u)
    tail = u[tm - SUBLANES:]
    ucarry[...] = tail
    utail_ref[0] = tail

    pf = _dot(xn, w_ref[:, o_m:o_f + LANES])
    qm_ref[0] = (pf[:, 0:mem_width] * scale).astype(BF16)
    logf = _log_sigmoid(pf[:, mem_width:mem_width + LANES] + bf_ref[...])
    logf_ref[0] = logf[:, 0:n_heads]

    lane = lax.broadcasted_iota(jnp.int32, (tm, LANES), 1)
    row = lax.broadcasted_iota(jnp.int32, (tm, LANES), 0)
    c = jnp.where(lane < n_heads, logf, 0.0)
    sh = 1
    while sh < tm:
        c = c + jnp.where(row >= sh, pltpu.roll(c, sh, 0), 0.0)
        sh *= 2
    c = c + ccarry[0:1]
    ccarry[...] = jnp.broadcast_to(c[tm - 1:tm], ccarry.shape)

    cs = c * LOG2E
    hi = cs.astype(BF16).astype(F32)
    r = cs - hi
    mid = r.astype(BF16).astype(F32)
    lo = r - mid
    parts = (hi + pltpu.roll(mid, n_heads, 1) + pltpu.roll(lo, 2 * n_heads, 1)
             + jnp.where(lane == 3 * n_heads, 1.0, 0.0)).astype(BF16)

    ek = _dot(parts, selk_ref[...])
    pk = _dot(xn, w_ref[:, o_k:o_m])
    k_ref[0] = pk
    half = lane < (LANES // 2)
    for hp in range(n_heads // 2):
        pair = pk[:, hp * LANES:(hp + 1) * LANES]
        rot = pltpu.roll(pair, LANES // 2, 1)
        e0 = ek[:, (2 * hp) * LANES:(2 * hp + 1) * LANES]
        e1 = ek[:, (2 * hp + 1) * LANES:(2 * hp + 2) * LANES]
        kaug_ref[0, 2 * hp] = jnp.where(half, pair, e0).astype(BF16)
        kaug_ref[0, 2 * hp + 1] = jnp.where(half, rot, e1).astype(BF16)

    qt = _dot_nt(wt_ref[0:fox_width, :], xn) * (scale * LOG2E)
    vt = _dot_nt(wt_ref[fox_width:2 * fox_width, :], xn)
    vt_ref[0] = vt
    ext = _dot_nt(selqt_ref[...], parts)
    pad = jnp.zeros((LANES - head_dim - SUBLANES, tm), F32)
    ones_rows = jnp.where(lax.broadcasted_iota(jnp.int32, (2 * SUBLANES, tm), 0) == 0, 1.0, 0.0)
    for h in range(n_heads):
        vtb_ref[0, h, 0] = jnp.concatenate([vt[h * head_dim:(h + 1) * head_dim], ones_rows], axis=0).astype(BF16)
        qaugt_ref[0, h, 0] = jnp.concatenate(
            [qt[h * head_dim:(h + 1) * head_dim], ext[h * SUBLANES:(h + 1) * SUBLANES], pad], axis=0).astype(BF16)


def _sel_matrices(n_heads, head_dim):
    selqt = np.zeros((n_heads * SUBLANES, LANES), np.float32)
    selk = np.zeros((LANES, n_heads * LANES), np.float32)
    for h in range(n_heads):
        base = h * LANES + head_dim
        for part in range(3):
            selqt[h * SUBLANES + part, part * n_heads + h] = 1.0
            selqt[h * SUBLANES + 3 + part, 3 * n_heads] = 1.0
            selk[3 * n_heads, base + part] = 1.0
            selk[part * n_heads + h, base + 3 + part] = -1.0
    return jnp.asarray(selqt, BF16), jnp.asarray(selk, BF16)


def _inproj_prompt(x, g, w, wt, bf, cw, selqt, selk, *, n_heads, head_dim, mem_width):
    B, S, D = x.shape
    cw_width = cw.shape[1]
    fox_width = n_heads * head_dim
    tm = ROW_TILE
    nt = S // tm
    kern = functools.partial(_inproj_prompt_kernel, cw_width=cw_width, fox_width=fox_width,
                             mem_width=mem_width, n_heads=n_heads, scale=head_dim ** -0.5)
    const = lambda a: pl.BlockSpec(a.shape, lambda b, t: (0,) * a.ndim)
    row_spec = lambda width: pl.BlockSpec((1, tm, width), lambda b, t: (b, t, 0))
    return pl.pallas_call(
        kern,
        grid=(B, nt),
        in_specs=[row_spec(D), const(g), const(w), const(wt), const(bf), const(cw), const(selqt), const(selk)],
        out_specs=[row_spec(cw_width),
                   pl.BlockSpec((1, n_heads, 1, LANES, tm), lambda b, t: (b, 0, t, 0, 0)),
                   pl.BlockSpec((1, n_heads, tm, LANES), lambda b, t: (b, 0, t, 0)),
                   row_spec(fox_width),
                   pl.BlockSpec((1, fox_width, tm), lambda b, t: (b, 0, t)),
                   pl.BlockSpec((1, n_heads, 1, head_dim + 2 * SUBLANES, tm), lambda b, t: (b, 0, t, 0, 0)),
                   row_spec(n_heads), row_spec(mem_width),
                   pl.BlockSpec((1, SUBLANES, cw_width), lambda b, t: (b, 0, 0))],
        out_shape=[jax.ShapeDtypeStruct((B, S, cw_width), F32),
                   jax.ShapeDtypeStruct((B, n_heads, nt, LANES, tm), BF16),
                   jax.ShapeDtypeStruct((B, n_heads, S, LANES), BF16),
                   jax.ShapeDtypeStruct((B, S, fox_width), F32),
                   jax.ShapeDtypeStruct((B, fox_width, S), F32),
                   jax.ShapeDtypeStruct((B, n_heads, nt, head_dim + 2 * SUBLANES, tm), BF16),
                   jax.ShapeDtypeStruct((B, S, n_heads), F32),
                   jax.ShapeDtypeStruct((B, S, mem_width), BF16),
                   jax.ShapeDtypeStruct((B, SUBLANES, cw_width), F32)],
        scratch_shapes=[pltpu.VMEM((SUBLANES, cw_width), F32), pltpu.VMEM((SUBLANES, LANES), F32)],
        compiler_params=pltpu.CompilerParams(dimension_semantics=("arbitrary", "arbitrary"),
                                             vmem_limit_bytes=VMEM_LIMIT),
    )(x, g, w, wt, bf, cw, selqt, selk)


def _fox_prompt_kernel(oi_ref, oj_ref, qt_ref, k_ref, vt_ref, o_ref, sta_sc, stb_sc, m_sc, acc_sc,
                       *, head_dim, n_below, unroll):
    n_q, tq = qt_ref.shape[2], qt_ref.shape[4]
    tk = tq
    m_sc[...] = jnp.full_like(m_sc, -jnp.inf)
    acc_sc[...] = jnp.zeros_like(acc_sc)
    bufs = (sta_sc, stb_sc)

    def scores(i, j, st_ref):
        off = pl.multiple_of(j * tk, tk)
        for hh in range(2):
            st_ref[hh] = _dot(k_ref[0, hh, pl.ds(off, tk), :], qt_ref[0, hh, i])

    def update(i, j, st_ref, masked):
        for hh in range(2):
            st = st_ref[hh]
            if masked:
                kpos = lax.broadcasted_iota(jnp.int32, st.shape, 0)
                qpos = lax.broadcasted_iota(jnp.int32, st.shape, 1)
                st = jnp.where(kpos <= qpos, st, -jnp.inf)
            m_old = m_sc[i, hh]
            m_new = jnp.maximum(m_old, jnp.max(st, axis=0, keepdims=True))
            p = jnp.exp2(st - m_new)
            acc_sc[i, hh] = jnp.exp2(m_old - m_new) * acc_sc[i, hh] + _dot(vt_ref[0, hh, j], p.astype(BF16))
            m_sc[i, hh] = m_new

    scores(oi_ref[0], oj_ref[0], sta_sc)

    def below(t, carry):
        for u in range(unroll):
            n = unroll * t + u
            scores(oi_ref[n + 1], oj_ref[n + 1], bufs[(u + 1) % 2])
            update(oi_ref[n], oj_ref[n], bufs[u % 2], False)
        return carry

    lax.fori_loop(0, n_below // unroll, below, 0)

    def finish(i):
        ot = jnp.concatenate([acc_sc[i, hh, 0:head_dim] / acc_sc[i, hh, head_dim:head_dim + 1] for hh in range(2)],
                             axis=0)
        o_ref[0, pl.ds(pl.multiple_of(i * tq, tq), tq), :] = ot.T

    def diagonal(t, carry):
        for u in range(2):
            i = 2 * t + u
            nxt = jnp.minimum(i + 1, n_q - 1)
            scores(nxt, nxt, bufs[(u + 1) % 2])
            update(i, i, bufs[u % 2], True)
            finish(i)
        return carry

    lax.fori_loop(0, n_q // 2, diagonal, 0)


def _fox_prompt(qaugt, kaug, vtb, *, head_dim):
    B, H, n_q, _, tq = qaugt.shape
    _, _, nt, v_rows, tk = vtb.shape
    S = n_q * tq
    assert tk == tq and nt == n_q and n_q % 2 == 0
    below = [(i, j) for i in range(n_q) for j in range(i)]
    unroll = 4 if len(below) % 4 == 0 else 2
    assert len(below) % unroll == 0
    sched = np.asarray(below + [(0, 0)], np.int32)
    grid_spec = pltpu.PrefetchScalarGridSpec(
        num_scalar_prefetch=2,
        grid=(B, H // 2),
        in_specs=[pl.BlockSpec((1, 2, n_q, LANES, tq), lambda b, hp, oi, oj: (b, hp, 0, 0, 0)),
                  pl.BlockSpec((1, 2, S, LANES), lambda b, hp, oi, oj: (b, hp, 0, 0)),
                  pl.BlockSpec((1, 2, nt, v_rows, tk), lambda b, hp, oi, oj: (b, hp, 0, 0, 0))],
        out_specs=pl.BlockSpec((1, S, 2 * head_dim), lambda b, hp, oi, oj: (b, 0, hp)),
        scratch_shapes=[pltpu.VMEM((2, tk, tq), F32), pltpu.VMEM((2, tk, tq), F32),
                        pltpu.VMEM((n_q, 2, 1, tq), F32), pltpu.VMEM((n_q, 2, v_rows, tq), F32)],
    )
    return pl.pallas_call(
        functools.partial(_fox_prompt_kernel, head_dim=head_dim, n_below=len(below), unroll=unroll),
        grid_spec=grid_spec,
        out_shape=jax.ShapeDtypeStruct((B, S, H * head_dim), F32),
        compiler_params=pltpu.CompilerParams(dimension_semantics=("arbitrary", "arbitrary"),
                                             vmem_limit_bytes=VMEM_LIMIT),
    )(jnp.asarray(sched[:, 0]), jnp.asarray(sched[:, 1]), qaugt, kaug, vtb)


def _mem_kv_kernel(mem_ref, g_ref, w_ref, mk_ref, mv_ref):
    h = (_normalize(mem_ref[0]) * g_ref[...]).astype(BF16)
    kv = _dot(h, w_ref[...])
    width = mk_ref.shape[2]
    mk_ref[0] = kv[:, 0:width]
    mv_ref[0] = kv[:, width:2 * width]


def _mem_kv(mem, g, w):
    B, M, D = mem.shape
    width = w.shape[1] // 2
    out = jax.ShapeDtypeStruct((B, M, width), F32)
    return pl.pallas_call(
        _mem_kv_kernel,
        grid=(B,),
        in_specs=[pl.BlockSpec((1, M, D), lambda b: (b, 0, 0)),
                  pl.BlockSpec(g.shape, lambda b: (0, 0)),
                  pl.BlockSpec(w.shape, lambda b: (0, 0))],
        out_specs=[pl.BlockSpec((1, M, width), lambda b: (b, 0, 0))] * 2,
        out_shape=[out, out],
        compiler_params=pltpu.CompilerParams(dimension_semantics=("arbitrary",)),
    )(mem, g, w)


def _mem_attn_prompt_kernel(q_ref, mk_ref, mv_ref, o_ref):
    tm = q_ref.shape[1]
    n_pairs = q_ref.shape[2] // LANES
    lane_k = lax.broadcasted_iota(jnp.int32, (mk_ref.shape[1], LANES), 1)
    lane_o = lax.broadcasted_iota(jnp.int32, (tm, LANES), 1)
    for hp in range(n_pairs):
        sl = slice(hp * LANES, (hp + 1) * LANES)
        q = q_ref[0, :, sl]
        mk = mk_ref[0, :, sl]
        mv = mv_ref[0, :, sl].astype(BF16)
        outs = []
        for hh in range(2):
            in_head = (lane_k < LANES // 2) if hh == 0 else (lane_k >= LANES // 2)
            s = _dot_nt(q, jnp.where(in_head, mk, 0.0).astype(BF16))
            p = jnp.exp(s - jnp.max(s, axis=-1, keepdims=True))
            l = jnp.sum(p, axis=-1, keepdims=True)
            outs.append(_dot(p.astype(BF16), mv) / l)
        o_ref[0, :, sl] = jnp.where(lane_o < LANES // 2, outs[0], outs[1])


def _mem_attn_prompt(qm, mk, mv):
    B, S, W = qm.shape
    M = mk.shape[1]
    tm = ROW_TILE
    return pl.pallas_call(
        _mem_attn_prompt_kernel,
        grid=(B, S // tm),
        in_specs=[pl.BlockSpec((1, tm, W), lambda b, t: (b, t, 0)),
                  pl.BlockSpec((1, M, W), lambda b, t: (b, 0, 0)),
                  pl.BlockSpec((1, M, W), lambda b, t: (b, 0, 0))],
        out_specs=pl.BlockSpec((1, tm, W), lambda b, t: (b, t, 0)),
        out_shape=jax.ShapeDtypeStruct((B, S, W), F32),
        compiler_params=pltpu.CompilerParams(dimension_semantics=("arbitrary", "arbitrary")),
    )(qm, mk, mv)


def _post_kernel(x_ref, yc_ref, yf_ref, ym_ref, og_ref, wo_ref, n2_ref, wg_ref, wu_ref, wd_ref, fg_ref, o_ref,
                 *, ff_split):
    og = og_ref[...]
    wc = yc_ref.shape[1]
    wf = yf_ref.shape[1]
    wm = ym_ref.shape[1]
    a = (_normalize(yc_ref[...]) * og[:, 0:wc]).astype(BF16)
    b = (_normalize(yf_ref[...]) * og[:, wc:wc + wf]).astype(BF16)
    c = (_normalize(ym_ref[...]) * og[:, wc + wf:wc + wf + wm]).astype(BF16)
    mix = (_dot(a, wo_ref[0:wc, :]) + _dot(b, wo_ref[wc:wc + wf, :])
           + _dot(c, wo_ref[wc + wf:wc + wf + wm, :]))
    x1 = x_ref[...] + mix
    h = (_normalize(x1) * n2_ref[...]).astype(BF16)
    d_ff = wg_ref.shape[1]
    ffn = jnp.zeros_like(x1)
    for sl in (slice(0, ff_split), slice(ff_split, d_ff)):
        gate = _dot(h, wg_ref[:, sl])
        up = _dot(h, wu_ref[:, sl])
        act = (gate * jax.nn.sigmoid(gate) * up).astype(BF16)
        ffn = ffn + _dot(act, wd_ref[sl, :])
    x2 = x1 + ffn
    o_ref[...] = _normalize(x2) * fg_ref[...]


def _post(x, yc, yf, ym, og, wo, n2, wg, wu, wd, fg, *, tm):
    N, D = x.shape
    d_ff = wg.shape[1]
    ff_split = pl.cdiv(d_ff // 2, MXU_WIDTH) * MXU_WIDTH
    const = lambda a: pl.BlockSpec(a.shape, lambda t: (0, 0), pipeline_mode=pl.Buffered(1))
    rows = lambda a: pl.BlockSpec((tm, a.shape[1]), lambda t: (t, 0))
    return pl.pallas_call(
        functools.partial(_post_kernel, ff_split=ff_split),
        grid=(N // tm,),
        in_specs=[rows(x), rows(yc), rows(yf), rows(ym), const(og), const(wo), const(n2), const(wg), const(wu),
                  const(wd), const(fg)],
        out_specs=pl.BlockSpec((tm, D), lambda t: (t, 0)),
        out_shape=jax.ShapeDtypeStruct((N, D), F32),
        compiler_params=pltpu.CompilerParams(dimension_semantics=("arbitrary",), vmem_limit_bytes=VMEM_LIMIT),
    )(x, yc, yf, ym, og, wo, n2, wg, wu, wd, fg)


def _inproj_sample_kernel(x_ref, g_ref, w_ref, bf_ref, cw_ref, h0_ref, h1_ref,
                          yconv_ref, u_ref, q_ref, k_ref, v_ref, logf_ref, qm_ref,
                          *, cw_width, fox_width, mem_width, scale):
    xn = (_normalize(x_ref[...]) * g_ref[...]).astype(BF16)
    p = _dot(xn, w_ref[...])
    o_q = 3 * cw_width
    o_k = o_q + fox_width
    o_v = o_k + fox_width
    o_m = o_v + fox_width
    o_f = o_m + mem_width
    b_c = p[:, 0:cw_width]
    u = p[:, cw_width:2 * cw_width] * p[:, 2 * cw_width:3 * cw_width]
    cw = cw_ref[...]
    yconv_ref[...] = b_c * (cw[0:1] * h0_ref[...] + cw[1:2] * h1_ref[...] + cw[2:3] * u)
    u_ref[...] = u
    q_ref[...] = p[:, o_q:o_k] * scale
    k_ref[...] = p[:, o_k:o_v]
    v_ref[...] = p[:, o_v:o_m]
    qm_ref[...] = p[:, o_m:o_f] * scale
    logf_ref[...] = _log_sigmoid(p[:, o_f:o_f + LANES] + bf_ref[...])


def _inproj_sample(x, g, w, bf, cw, h0, h1, *, n_heads, head_dim, mem_width):
    N, D = x.shape
    cw_width = cw.shape[1]
    fox_width = n_heads * head_dim
    kern = functools.partial(_inproj_sample_kernel, cw_width=cw_width, fox_width=fox_width,
                             mem_width=mem_width, scale=head_dim ** -0.5)
    sds = lambda width: jax.ShapeDtypeStruct((N, width), F32)
    return pl.pallas_call(
        kern,
        out_shape=[sds(cw_width), sds(cw_width), sds(fox_width), sds(fox_width), sds(fox_width), sds(LANES),
                   sds(mem_width)],
        compiler_params=pltpu.CompilerParams(vmem_limit_bytes=VMEM_LIMIT),
    )(x, g, w, bf, cw, h0, h1)


NEG_BIG = -1e30


def _tile_sum(x):
    return jnp.sum(x.reshape(x.shape[0] // SUBLANES, SUBLANES, x.shape[1]), axis=0)


def _sublane_allsum(x):
    x = x + pltpu.roll(x, 4, 0)
    x = x + pltpu.roll(x, 2, 0)
    return x + pltpu.roll(x, 1, 0)


def _scale_cols(x, w):
    r, l = x.shape
    return (x.reshape(r // SUBLANES, SUBLANES, l) * w[None]).reshape(r, l)


def _lane_sum_to_row(x, ones):
    hi, mid, lo = _split3(x)
    return (_dot_nt(ones, hi) + _dot_nt(ones, mid) + _dot_nt(ones, lo))[0:1]


def _fox_decode_kernel(pt_ref, qb_ref, knb_ref, vnb_ref, lfn_ref, scan_ref, ones_ref, *rest, n_pages_step):
    g = n_pages_step
    k_refs = rest[0:g]
    v_refs = rest[g:2 * g]
    lf_refs = rest[2 * g:3 * g]
    o_ref = rest[3 * g]
    m_sc, l_sc, r_sc, acc_sc = rest[3 * g + 1:]
    n = pl.program_id(1)
    _, n_heads, head_dim, page = qb_ref.shape

    @pl.when(n == 0)
    def _():
        first = lax.broadcasted_iota(jnp.int32, (SUBLANES, page), 1) == 0
        first_d = lax.broadcasted_iota(jnp.int32, (head_dim, page), 1) == 0
        for h in range(n_heads):
            s_new = _sublane_allsum(_tile_sum(qb_ref[0, h] * knb_ref[0, h]))
            m_sc[h] = jnp.where(first, s_new, NEG_BIG)
            l_sc[h] = jnp.where(first, 1.0, 0.0)
            acc_sc[h] = jnp.where(first_d, vnb_ref[0, h], 0.0)
        r_sc[...] = lfn_ref[0]

    lf_all = jnp.concatenate([lf_refs[t][0] for t in range(g)], axis=0)
    scans = _dot_exact_lhs(lf_all, scan_ref[...])
    r = r_sc[...]
    biases = []
    for t in range(g):
        rows = slice(t * n_heads, (t + 1) * n_heads)
        biases.append(r + scans[rows, 0:page])
        r = r + scans[rows, page:2 * page]
    r_sc[...] = r

    for h in range(n_heads):
        qb = qb_ref[0, h]
        ss = []
        for t in range(g):
            s = _sublane_allsum(_tile_sum(k_refs[t][0, h] * qb))
            ss.append(s + jnp.broadcast_to(biases[t][h:h + 1], s.shape))
        m_old = m_sc[h]
        m_new = functools.reduce(jnp.maximum, ss, m_old)
        alpha = jnp.exp(m_old - m_new)
        l = l_sc[h] * alpha
        acc = _scale_cols(acc_sc[h], alpha)
        for t in range(g):
            p = jnp.exp(ss[t] - m_new)
            l = l + p
            acc = acc + _scale_cols(v_refs[t][0, h], p)
        m_sc[h] = m_new
        l_sc[h] = l
        acc_sc[h] = acc

    @pl.when(n == pl.num_programs(1) - 1)
    def _():
        for h in range(n_heads):
            m = m_sc[h]
            w = jnp.exp(m - jnp.max(m, axis=-1, keepdims=True))
            l_tot = jnp.sum(l_sc[h] * w, axis=-1, keepdims=True)
            o = _lane_sum_to_row(_scale_cols(acc_sc[h], w), ones_ref[...])
            o_ref[0, h:h + 1, :] = o / l_tot[0:1]


def _fox_decode(page_table, qb, knb, vnb, lfn, cache_kt, cache_vt, cache_lft):
    Bd, n_pages = page_table.shape
    g = PAGES_PER_STEP
    _, n_heads, head_dim, page = cache_kt.shape
    pt_flat = page_table.reshape(-1)
    scan = jnp.asarray(np.concatenate([np.tril(np.ones((page, page), np.float32), -1),
                                       np.ones((page, page), np.float32)], axis=1), BF16)
    ones = jnp.ones((2 * SUBLANES, page), BF16)

    def page_spec(shape, t):
        def index_map(b, n, pt):
            return (pt[b * n_pages + n_pages - 1 - (n * g + t)],) + (0,) * (len(shape) - 1)
        return pl.BlockSpec(shape, index_map)

    per_b = lambda a: pl.BlockSpec((1,) + a.shape[1:], lambda b, n, pt: (b,) + (0,) * (a.ndim - 1))
    const = lambda a: pl.BlockSpec(a.shape, lambda b, n, pt: (0,) * a.ndim)
    in_specs = ([per_b(qb), per_b(knb), per_b(vnb), per_b(lfn), const(scan), const(ones)]
                + [page_spec((1, n_heads, head_dim, page), t) for t in range(g)]
                + [page_spec((1, n_heads, head_dim, page), t) for t in range(g)]
                + [page_spec((1, n_heads, page), t) for t in range(g)])
    grid_spec = pltpu.PrefetchScalarGridSpec(
        num_scalar_prefetch=1,
        grid=(Bd, n_pages // g),
        in_specs=in_specs,
        out_specs=pl.BlockSpec((1, n_heads, head_dim), lambda b, n, pt: (b, 0, 0)),
        scratch_shapes=[pltpu.VMEM((n_heads, SUBLANES, page), F32), pltpu.VMEM((n_heads, SUBLANES, page), F32),
                        pltpu.VMEM((n_heads, page), F32), pltpu.VMEM((n_heads, head_dim, page), F32)],
    )
    return pl.pallas_call(
        functools.partial(_fox_decode_kernel, n_pages_step=g),
        grid_spec=grid_spec,
        out_shape=jax.ShapeDtypeStruct((Bd, n_heads, head_dim), F32),
        compiler_params=pltpu.CompilerParams(dimension_semantics=("arbitrary", "arbitrary"),
                                             vmem_limit_bytes=VMEM_LIMIT),
    )(pt_flat, qb, knb, vnb, lfn, scan, ones, *([cache_kt] * g), *([cache_vt] * g), *([cache_lft] * g))


def _mem_attn_sample_kernel(qb_ref, mk_ref, mv_ref, ones_ref, o_ref):
    n_heads = qb_ref.shape[1]
    for h in range(n_heads):
        s = _sublane_allsum(_tile_sum(mk_ref[0, h] * qb_ref[0, h]))
        p = jnp.exp(s - jnp.max(s, axis=-1, keepdims=True))
        l = jnp.sum(p, axis=-1, keepdims=True)
        o = _lane_sum_to_row(_scale_cols(mv_ref[0, h], p), ones_ref[...])
        o_ref[0, h:h + 1, :] = o / l[0:1]


def _mem_attn_sample(qb, mkt, mvt):
    Bd, n_heads, head_dim, m_len = mkt.shape
    ones = jnp.ones((2 * SUBLANES, m_len), BF16)
    per_b = pl.BlockSpec((1, n_heads, head_dim, m_len), lambda b: (b, 0, 0, 0))
    return pl.pallas_call(
        _mem_attn_sample_kernel,
        grid=(Bd,),
        in_specs=[per_b, per_b, per_b, pl.BlockSpec(ones.shape, lambda b: (0, 0))],
        out_specs=pl.BlockSpec((1, n_heads, head_dim), lambda b: (b, 0, 0)),
        out_shape=jax.ShapeDtypeStruct((Bd, n_heads, head_dim), F32),
        compiler_params=pltpu.CompilerParams(dimension_semantics=("arbitrary",)),
    )(qb, mkt, mvt, ones)


def _lane_broadcast(x, n):
    return jnp.broadcast_to(x[..., None], x.shape + (n,))


def kernel(x_prompt, x_sample, mem_prompt, cache_k, cache_v, cache_logf, state_conv, cache_mem_k, cache_mem_v,
           page_table, norm1_g, w_in, b_f, conv_w, mem_norm_g, w_mem_kv, out_norm_g, w_out, norm2_g,
           w_gate, w_up, w_down, final_norm_g):
    B, S, D = x_prompt.shape
    Bd, Td, _ = x_sample.shape
    depth, n_pool, page, n_heads, head_dim = cache_k.shape
    mem_heads = cache_mem_k.shape[3]
    mem_len = cache_mem_k.shape[2]
    conv_k, cw_width = conv_w.shape[1], conv_w.shape[2]
    fox_width = n_heads * head_dim
    mem_width = mem_heads * head_dim
    assert depth == 1, "one layer: the fused tail applies the final norm"
    assert Td == 1 and conv_k == 3 and head_dim * 2 == LANES and n_heads % 2 == 0 and mem_heads % 2 == 0
    assert S % ROW_TILE == 0 and page_table.shape[1] % PAGES_PER_STEP == 0 and 3 * n_heads < LANES
    assert page % LANES == 0 and mem_len % LANES == 0

    selqt, selk = _sel_matrices(n_heads, head_dim)
    row2 = lambda a: a.reshape(1, -1)
    xs = x_sample.reshape(Bd, D)

    offs = np.cumsum([0, cw_width, cw_width, cw_width, fox_width, fox_width, fox_width, n_heads, mem_width])
    wl = w_in[0]
    w_f = jnp.pad(wl[:, offs[6]:offs[7]], ((0, 0), (0, LANES - n_heads)))
    w_cat = jnp.concatenate([wl[:, :offs[6]], wl[:, offs[7]:offs[8]], w_f], axis=1).astype(BF16)
    w_nat = jnp.concatenate([wl[:, :offs[3]], wl[:, offs[4]:offs[5]], wl[:, offs[7]:offs[8]], w_f],
                            axis=1).astype(BF16)
    w_t = jnp.concatenate([wl[:, offs[3]:offs[4]], wl[:, offs[5]:offs[6]]], axis=1).T.astype(BF16)
    bf_pad = jnp.pad(b_f[0], (0, LANES - n_heads)).reshape(1, LANES)
    g1 = row2(norm1_g[0])
    post_args = (row2(out_norm_g[0]), w_out[0].astype(BF16), row2(norm2_g[0]), w_gate[0].astype(BF16),
                 w_up[0].astype(BF16), w_down[0].astype(BF16), row2(final_norm_g))

    yconv, qaugt, kaug, k_p, vt_p, vtb, logf_p, qm, utail = _inproj_prompt(
        x_prompt, g1, w_nat, w_t, bf_pad, conv_w[0], selqt, selk,
        n_heads=n_heads, head_dim=head_dim, mem_width=mem_width)
    yfox = _fox_prompt(qaugt, kaug, vtb, head_dim=head_dim)
    v_p = jnp.transpose(vt_p.reshape(B, n_heads, head_dim, S), (0, 3, 1, 2))
    mk, mv = _mem_kv(mem_prompt, row2(mem_norm_g[0]), w_mem_kv[0].astype(BF16))
    ymem = _mem_attn_prompt(qm, mk, mv)
    y_prompt = _post(x_prompt.reshape(B * S, D), yconv.reshape(B * S, cw_width), yfox.reshape(B * S, fox_width),
                     ymem.reshape(B * S, mem_width), *post_args, tm=ROW_TILE)

    yconv_s, u_s, q_s, k_s, v_s, logf_s, qm_s = _inproj_sample(
        xs, g1, w_cat, bf_pad, conv_w[0], state_conv[0, :, 0], state_conv[0, :, 1],
        n_heads=n_heads, head_dim=head_dim, mem_width=mem_width)
    per_head = lambda a, nh: a.reshape(Bd, nh, head_dim)
    yfox_s = _fox_decode(page_table,
                         _lane_broadcast(per_head(q_s, n_heads), page), _lane_broadcast(per_head(k_s, n_heads), page),
                         _lane_broadcast(per_head(v_s, n_heads), page), _lane_broadcast(logf_s[:, :n_heads], page),
                         jnp.transpose(cache_k[0], (0, 2, 3, 1)), jnp.transpose(cache_v[0], (0, 2, 3, 1)),
                         jnp.transpose(cache_logf[0], (0, 2, 1)))
    ymem_s = _mem_attn_sample(_lane_broadcast(per_head(qm_s, mem_heads), mem_len),
                              jnp.transpose(cache_mem_k[0], (0, 2, 3, 1)), jnp.transpose(cache_mem_v[0], (0, 2, 3, 1)))
    y_sample = _post(xs, yconv_s, yfox_s.reshape(Bd, fox_width), ymem_s.reshape(Bd, mem_width), *post_args, tm=Bd)

    return (y_prompt.reshape(B, S, D), y_sample.reshape(Bd, 1, D),
            k_p.reshape(1, B, S, n_heads, head_dim), v_p.reshape(1, B, S, n_heads, head_dim), logf_p[None],
            utail[None, :, SUBLANES - (conv_k - 1):],
            mk.reshape(1, B, mem_len, mem_heads, head_dim), mv.reshape(1, B, mem_len, mem_heads, head_dim),
            k_s.reshape(1, Bd, 1, n_heads, head_dim), v_s.reshape(1, Bd, 1, n_heads, head_dim),
            logf_s[None, :, None, :n_heads], jnp.stack([state_conv[0, :, 1], u_s], axis=1)[None])
```

```python
import functools

import numpy as np
import jax
import jax.numpy as jnp
from jax import lax
from jax.experimental import pallas as pl
from jax.experimental.pallas import tpu as pltpu

F32 = jnp.float32
BF16 = jnp.bfloat16
EPS = 1e-6
LOG2E = 1.4426950408889634

LANES = 128
SUBLANES = 8
ROW_TILE = 512
MXU_WIDTH = 256
VMEM_LIMIT = 56 * 1024 * 1024


def _dot(a, b):
    return jnp.dot(a, b, preferred_element_type=F32)


def _dot_nt(a, b):
    return lax.dot_general(a, b, (((1,), (1,)), ((), ())), preferred_element_type=F32)


def _normalize(x):
    return x * lax.rsqrt(jnp.mean(x * x, axis=-1, keepdims=True) + EPS)


def _split3(x):
    hi = x.astype(BF16)
    r = x - hi.astype(F32)
    mid = r.astype(BF16)
    lo = (r - mid.astype(F32)).astype(BF16)
    return hi, mid, lo


def _dot_exact_lhs(x, m):
    hi, mid, lo = _split3(x)
    return _dot(hi, m) + _dot(mid, m) + _dot(lo, m)


def _log_sigmoid(x):
    return jnp.minimum(x, 0.0) - jnp.log1p(jnp.exp(-jnp.abs(x)))


def _inproj_prompt_kernel(x_ref, g_ref, w_ref, wt_ref, bf_ref, cw_ref, selqt_ref, selk_ref,
                          yconv_ref, qaugt_ref, kaug_ref, k_ref, vt_ref, vtb_ref, logf_ref, qm_ref, utail_ref,
                          ucarry, ccarry, *, cw_width, fox_width, mem_width, n_heads, scale):
    t = pl.program_id(1)
    tm = x_ref.shape[1]
    head_dim = fox_width // n_heads

    @pl.when(t == 0)
    def _():
        ucarry[...] = jnp.zeros_like(ucarry)
        ccarry[...] = jnp.zeros_like(ccarry)

    x = x_ref[0]
    xn = (_normalize(x) * g_ref[...]).astype(BF16)

    o_k = 3 * cw_width
    o_m = o_k + fox_width
    o_f = o_m + mem_width

    pc = _dot(xn, w_ref[:, 0:o_k])
    b_c = pc[:, 0:cw_width]
    u = pc[:, cw_width:2 * cw_width] * pc[:, 2 * cw_width:3 * cw_width]
    prev = ucarry[...]
    row8 = lax.broadcasted_iota(jnp.int32, (SUBLANES, cw_width), 0)
    r1 = pltpu.roll(u, 1, 0)
    r2 = pltpu.roll(u, 2, 0)
    h1 = jnp.where(row8 < 1, pltpu.roll(prev, 1, 0), r1[0:SUBLANES])
    h2 = jnp.where(row8 < 2, pltpu.roll(prev, 2, 0), r2[0:SUBLANES])
    u1 = jnp.concatenate([h1, r1[SUBLANES:]], axis=0)
    u2 = jnp.concatenate([h2, r2[SUBLANES:]], axis=0)
    cw = cw_ref[...]
    yconv_ref[0] = b_c * (cw[0:1] * u2 + cw[1:2] * u1 + cw[2:3] * u)
    tail = u[tm - SUBLANES:]
    ucarry[...] = tail
    utail_ref[0] = tail

    pf = _dot(xn, w_ref[:, o_m:o_f + LANES])
    qm_ref[0] = (pf[:, 0:mem_width] * scale).astype(BF16)
    logf = _log_sigmoid(pf[:, mem_width:mem_width + LANES] + bf_ref[...])
    logf_ref[0] = logf[:, 0:n_heads]

    lane = lax.broadcasted_iota(jnp.int32, (tm, LANES), 1)
    row = lax.broadcasted_iota(jnp.int32, (tm, LANES), 0)
    c = jnp.where(lane < n_heads, logf, 0.0)
    sh = 1
    while sh < tm:
        c = c + jnp.where(row >= sh, pltpu.roll(c, sh, 0), 0.0)
        sh *= 2
    c = c + ccarry[0:1]
    ccarry[...] = jnp.broadcast_to(c[tm - 1:tm], ccarry.shape)

    cs = c * LOG2E
    hi = cs.astype(BF16).astype(F32)
    r = cs - hi
    mid = r.astype(BF16).astype(F32)
    lo = r - mid
    parts = (hi + pltpu.roll(mid, n_heads, 1) + pltpu.roll(lo, 2 * n_heads, 1)
             + jnp.where(lane == 3 * n_heads, 1.0, 0.0)).astype(BF16)

    ek = _dot(parts, selk_ref[...])
    pk = _dot(xn, w_ref[:, o_k:o_m])
    k_ref[0] = pk
    half = lane < (LANES // 2)
    for hp in range(n_heads // 2):
        pair = pk[:, hp * LANES:(hp + 1) * LANES]
        rot = pltpu.roll(pair, LANES // 2, 1)
        e0 = ek[:, (2 * hp) * LANES:(2 * hp + 1) * LANES]
        e1 = ek[:, (2 * hp + 1) * LANES:(2 * hp + 2) * LANES]
        kaug_ref[0, 2 * hp] = jnp.where(half, pair, e0).astype(BF16)
        kaug_ref[0, 2 * hp + 1] = jnp.where(half, rot, e1).astype(BF16)

    qt = _dot_nt(wt_ref[0:fox_width, :], xn) * (scale * LOG2E)
    vt = _dot_nt(wt_ref[fox_width:2 * fox_width, :], xn)
    vt_ref[0] = vt
    ext = _dot_nt(selqt_ref[...], parts)
    pad = jnp.zeros((LANES - head_dim - SUBLANES, tm), F32)
    ones_rows = jnp.where(lax.broadcasted_iota(jnp.int32, (2 * SUBLANES, tm), 0) == 0, 1.0, 0.0)
    for h in range(n_heads):
        vtb_ref[0, h, 0] = jnp.concatenate([vt[h * head_dim:(h + 1) * head_dim], ones_rows], axis=0).astype(BF16)
        qaugt_ref[0, h, 0] = jnp.concatenate(
            [qt[h * head_dim:(h + 1) * head_dim], ext[h * SUBLANES:(h + 1) * SUBLANES], pad], axis=0).astype(BF16)


def _sel_matrices(n_heads, head_dim):
    selqt = np.zeros((n_heads * SUBLANES, LANES), np.float32)
    selk = np.zeros((LANES, n_heads * LANES), np.float32)
    for h in range(n_heads):
        base = h * LANES + head_dim
        for part in range(3):
            selqt[h * SUBLANES + part, part * n_heads + h] = 1.0
            selqt[h * SUBLANES + 3 + part, 3 * n_heads] = 1.0
            selk[3 * n_heads, base + part] = 1.0
            selk[part * n_heads + h, base + 3 + part] = -1.0
    return jnp.asarray(selqt, BF16), jnp.asarray(selk, BF16)


def _inproj_prompt(x, g, w, wt, bf, cw, selqt, selk, *, n_heads, head_dim, mem_width):
    B, S, D = x.shape
    cw_width = cw.shape[1]
    fox_width = n_heads * head_dim
    tm = ROW_TILE
    nt = S // tm
    kern = functools.partial(_inproj_prompt_kernel, cw_width=cw_width, fox_width=fox_width,
                             mem_width=mem_width, n_heads=n_heads, scale=head_dim ** -0.5)
    const = lambda a: pl.BlockSpec(a.shape, lambda b, t: (0,) * a.ndim)
    row_spec = lambda width: pl.BlockSpec((1, tm, width), lambda b, t: (b, t, 0))
    return pl.pallas_call(
        kern,
        grid=(B, nt),
        in_specs=[row_spec(D), const(g), const(w), const(wt), const(bf), const(cw), const(selqt), const(selk)],
        out_specs=[row_spec(cw_width),
                   pl.BlockSpec((1, n_heads, 1, LANES, tm), lambda b, t: (b, 0, t, 0, 0)),
                   pl.BlockSpec((1, n_heads, tm, LANES), lambda b, t: (b, 0, t, 0)),
                   row_spec(fox_width),
                   pl.BlockSpec((1, fox_width, tm), lambda b, t: (b, 0, t)),
                   pl.BlockSpec((1, n_heads, 1, head_dim + 2 * SUBLANES, tm), lambda b, t: (b, 0, t, 0, 0)),
                   row_spec(n_heads), row_spec(mem_width),
                   pl.BlockSpec((1, SUBLANES, cw_width), lambda b, t: (b, 0, 0))],
        out_shape=[jax.ShapeDtypeStruct((B, S, cw_width), F32),
                   jax.ShapeDtypeStruct((B, n_heads, nt, LANES, tm), BF16),
                   jax.ShapeDtypeStruct((B, n_heads, S, LANES), BF16),
                   jax.ShapeDtypeStruct((B, S, fox_width), F32),
                   jax.ShapeDtypeStruct((B, fox_width, S), F32),
                   jax.ShapeDtypeStruct((B, n_heads, nt, head_dim + 2 * SUBLANES, tm), BF16),
                   jax.ShapeDtypeStruct((B, S, n_heads), F32),
                   jax.ShapeDtypeStruct((B, S, mem_width), BF16),
                   jax.ShapeDtypeStruct((B, SUBLANES, cw_width), F32)],
        scratch_shapes=[pltpu.VMEM((SUBLANES, cw_width), F32), pltpu.VMEM((SUBLANES, LANES), F32)],
        compiler_params=pltpu.CompilerParams(dimension_semantics=("arbitrary", "arbitrary"),
                                             vmem_limit_bytes=VMEM_LIMIT),
    )(x, g, w, wt, bf, cw, selqt, selk)


def _mem_kv_kernel(mem_ref, g_ref, w_ref, mk_ref, mv_ref):
    h = (_normalize(mem_ref[0]) * g_ref[...]).astype(BF16)
    kv = _dot(h, w_ref[...])
    width = mk_ref.shape[2]
    mk_ref[0] = kv[:, 0:width]
    mv_ref[0] = kv[:, width:2 * width]


def _mem_kv(mem, g, w):
    B, M, D = mem.shape
    width = w.shape[1] // 2
    out = jax.ShapeDtypeStruct((B, M, width), F32)
    return pl.pallas_call(
        _mem_kv_kernel,
        grid=(B,),
        in_specs=[pl.BlockSpec((1, M, D), lambda b: (b, 0, 0)),
                  pl.BlockSpec(g.shape, lambda b: (0, 0)),
                  pl.BlockSpec(w.shape, lambda b: (0, 0))],
        out_specs=[pl.BlockSpec((1, M, width), lambda b: (b, 0, 0))] * 2,
        out_shape=[out, out],
        compiler_params=pltpu.CompilerParams(dimension_semantics=("arbitrary",)),
    )(mem, g, w)


def _mem_attn_prompt_kernel(q_ref, mk_ref, mv_ref, o_ref):
    tm = q_ref.shape[1]
    n_pairs = q_ref.shape[2] // LANES
    lane_k = lax.broadcasted_iota(jnp.int32, (mk_ref.shape[1], LANES), 1)
    lane_o = lax.broadcasted_iota(jnp.int32, (tm, LANES), 1)
    for hp in range(n_pairs):
        sl = slice(hp * LANES, (hp + 1) * LANES)
        q = q_ref[0, :, sl]
        mk = mk_ref[0, :, sl]
        mv = mv_ref[0, :, sl].astype(BF16)
        outs = []
        for hh in range(2):
            in_head = (lane_k < LANES // 2) if hh == 0 else (lane_k >= LANES // 2)
            s = _dot_nt(q, jnp.where(in_head, mk, 0.0).astype(BF16))
            p = jnp.exp(s - jnp.max(s, axis=-1, keepdims=True))
            l = jnp.sum(p, axis=-1, keepdims=True)
            outs.append(_dot(p.astype(BF16), mv) / l)
        o_ref[0, :, sl] = jnp.where(lane_o < LANES // 2, outs[0], outs[1])


def _mem_attn_prompt(qm, mk, mv):
    B, S, W = qm.shape
    M = mk.shape[1]
    tm = ROW_TILE
    return pl.pallas_call(
        _mem_attn_prompt_kernel,
        grid=(B, S // tm),
        in_specs=[pl.BlockSpec((1, tm, W), lambda b, t: (b, t, 0)),
                  pl.BlockSpec((1, M, W), lambda b, t: (b, 0, 0)),
                  pl.BlockSpec((1, M, W), lambda b, t: (b, 0, 0))],
        out_specs=pl.BlockSpec((1, tm, W), lambda b, t: (b, t, 0)),
        out_shape=jax.ShapeDtypeStruct((B, S, W), F32),
        compiler_params=pltpu.CompilerParams(dimension_semantics=("arbitrary", "arbitrary")),
    )(qm, mk, mv)


def _post_kernel(x_ref, yc_ref, yf_ref, ym_ref, og_ref, wo_ref, n2_ref, wg_ref, wu_ref, wd_ref, fg_ref, o_ref,
                 *, ff_split):
    og = og_ref[...]
    wc = yc_ref.shape[1]
    wf = yf_ref.shape[1]
    wm = ym_ref.shape[1]
    a = (_normalize(yc_ref[...]) * og[:, 0:wc]).astype(BF16)
    b = (_normalize(yf_ref[...]) * og[:, wc:wc + wf]).astype(BF16)
    c = (_normalize(ym_ref[...]) * og[:, wc + wf:wc + wf + wm]).astype(BF16)
    mix = (_dot(a, wo_ref[0:wc, :]) + _dot(b, wo_ref[wc:wc + wf, :])
           + _dot(c, wo_ref[wc + wf:wc + wf + wm, :]))
    x1 = x_ref[...] + mix
    h = (_normalize(x1) * n2_ref[...]).astype(BF16)
    d_ff = wg_ref.shape[1]
    ffn = jnp.zeros_like(x1)
    for sl in (slice(0, ff_split), slice(ff_split, d_ff)):
        gate = _dot(h, wg_ref[:, sl])
        up = _dot(h, wu_ref[:, sl])
        act = (gate * jax.nn.sigmoid(gate) * up).astype(BF16)
        ffn = ffn + _dot(act, wd_ref[sl, :])
    x2 = x1 + ffn
    o_ref[...] = _normalize(x2) * fg_ref[...]


def _post(x, yc, yf, ym, og, wo, n2, wg, wu, wd, fg, *, tm):
    N, D = x.shape
    d_ff = wg.shape[1]
    ff_split = pl.cdiv(d_ff // 2, MXU_WIDTH) * MXU_WIDTH
    const = lambda a: pl.BlockSpec(a.shape, lambda t: (0, 0), pipeline_mode=pl.Buffered(1))
    rows = lambda a: pl.BlockSpec((tm, a.shape[1]), lambda t: (t, 0))
    return pl.pallas_call(
        functools.partial(_post_kernel, ff_split=ff_split),
        grid=(N // tm,),
        in_specs=[rows(x), rows(yc), rows(yf), rows(ym), const(og), const(wo), const(n2), const(wg), const(wu),
                  const(wd), const(fg)],
        out_specs=pl.BlockSpec((tm, D), lambda t: (t, 0)),
        out_shape=jax.ShapeDtypeStruct((N, D), F32),
        compiler_params=pltpu.CompilerParams(dimension_semantics=("arbitrary",), vmem_limit_bytes=VMEM_LIMIT),
    )(x, yc, yf, ym, og, wo, n2, wg, wu, wd, fg)


def _inproj_sample_kernel(x_ref, g_ref, w_ref, bf_ref, cw_ref, h0_ref, h1_ref,
                          yconv_ref, u_ref, q_ref, k_ref, v_ref, logf_ref, qm_ref,
                          *, cw_width, fox_width, mem_width, scale):
    xn = (_normalize(x_ref[...]) * g_ref[...]).astype(BF16)
    p = _dot(xn, w_ref[...])
    o_q = 3 * cw_width
    o_k = o_q + fox_width
    o_v = o_k + fox_width
    o_m = o_v + fox_width
    o_f = o_m + mem_width
    b_c = p[:, 0:cw_width]
    u = p[:, cw_width:2 * cw_width] * p[:, 2 * cw_width:3 * cw_width]
    cw = cw_ref[...]
    yconv_ref[...] = b_c * (cw[0:1] * h0_ref[...] + cw[1:2] * h1_ref[...] + cw[2:3] * u)
    u_ref[...] = u
    q_ref[...] = p[:, o_q:o_k] * scale
    k_ref[...] = p[:, o_k:o_v]
    v_ref[...] = p[:, o_v:o_m]
    qm_ref[...] = p[:, o_m:o_f] * scale
    logf_ref[...] = _log_sigmoid(p[:, o_f:o_f + LANES] + bf_ref[...])


def _inproj_sample(x, g, w, bf, cw, h0, h1, *, n_heads, head_dim, mem_width):
    N, D = x.shape
    cw_width = cw.shape[1]
    fox_width = n_heads * head_dim
    kern = functools.partial(_inproj_sample_kernel, cw_width=cw_width, fox_width=fox_width,
                             mem_width=mem_width, scale=head_dim ** -0.5)
    sds = lambda width: jax.ShapeDtypeStruct((N, width), F32)
    return pl.pallas_call(
        kern,
        out_shape=[sds(cw_width), sds(cw_width), sds(fox_width), sds(fox_width), sds(fox_width), sds(LANES),
                   sds(mem_width)],
        compiler_params=pltpu.CompilerParams(vmem_limit_bytes=VMEM_LIMIT),
    )(x, g, w, bf, cw, h0, h1)


NEG_BIG = -1e30


def _tile_sum(x):
    return jnp.sum(x.reshape(x.shape[0] // SUBLANES, SUBLANES, x.shape[1]), axis=0)


def _sublane_allsum(x):
    x = x + pltpu.roll(x, 4, 0)
    x = x + pltpu.roll(x, 2, 0)
    return x + pltpu.roll(x, 1, 0)


def _scale_cols(x, w):
    r, l = x.shape
    return (x.reshape(r // SUBLANES, SUBLANES, l) * w[None]).reshape(r, l)


def _lane_sum_to_row(x, ones):
    hi, mid, lo = _split3(x)
    return (_dot_nt(ones, hi) + _dot_nt(ones, mid) + _dot_nt(ones, lo))[0:1]


def _fox_kernel(oi_ref, oj_ref, pt_ref,
                qt_ref, k_ref, vt_ref, qb_ref, knb_ref, vnb_ref, lfn_ref, scan_ref, ones_ref,
                ck_hbm, cv_hbm, clf_hbm,
                o_ref, od_ref,
                sta_sc, stb_sc, m_sc, acc_sc,
                kbuf, vbuf, lfbuf, ksem, vsem, lfsem, md_sc, ld_sc, rd_sc, accd_sc,
                *, head_dim, n_below, unroll, n_pages, pages_below, pages_diag):
    n_q, tq = qt_ref.shape[2], qt_ref.shape[4]
    tk = tq
    n_seq, n_heads, _, page = qb_ref.shape
    trips_below = n_below // unroll
    trips_diag = n_q // 2
    step = pl.program_id(0) * pl.num_programs(1) + pl.program_id(1)
    bufs = (sta_sc, stb_sc)

    def page_copies(p0, count, slot):
        seq = step * n_seq + p0 // n_pages
        newest = seq * n_pages + (n_pages - 1) - lax.rem(p0, n_pages)
        copies = []
        for u in range(count):
            pool_page = pt_ref[newest - u]
            copies += [pltpu.make_async_copy(ck_hbm.at[pool_page], kbuf.at[slot, u], ksem.at[slot]),
                       pltpu.make_async_copy(cv_hbm.at[pool_page], vbuf.at[slot, u], vsem.at[slot]),
                       pltpu.make_async_copy(clf_hbm.at[pool_page], lfbuf.at[slot, u], lfsem.at[slot])]
        return copies

    def start_pages(p0, count, slot):
        for c in page_copies(p0, count, slot):
            c.start()

    def wait_pages(p0, count, slot):
        for c in page_copies(p0, count, slot):
            c.wait()

    def decode_pages(p0, count, slot):
        seq = p0 // n_pages
        lf_all = jnp.concatenate([lfbuf[slot, u] for u in range(count)], axis=0)
        scans = _dot_exact_lhs(lf_all, scan_ref[...])
        r = rd_sc[seq]
        biases = []
        for u in range(count):
            rows = slice(u * n_heads, (u + 1) * n_heads)
            biases.append(r + scans[rows, 0:page])
            r = r + scans[rows, page:2 * page]
        rd_sc[seq] = r
        for h in range(n_heads):
            qb = qb_ref[seq, h]
            ss = []
            for u in range(count):
                s = _sublane_allsum(_tile_sum(kbuf[slot, u, h] * qb))
                ss.append(s + jnp.broadcast_to(biases[u][h:h + 1], s.shape))
            m_old = md_sc[seq, h]
            m_new = functools.reduce(jnp.maximum, ss, m_old)
            alpha = jnp.exp(m_old - m_new)
            l = ld_sc[seq, h] * alpha
            acc = _scale_cols(accd_sc[seq, h], alpha)
            for u in range(count):
                p = jnp.exp(ss[u] - m_new)
                l = l + p
                acc = acc + _scale_cols(vbuf[slot, u, h], p)
            md_sc[seq, h] = m_new
            ld_sc[seq, h] = l
            accd_sc[seq, h] = acc

    start_pages(0, pages_below, 0)

    first = lax.broadcasted_iota(jnp.int32, (SUBLANES, page), 1) == 0
    first_d = lax.broadcasted_iota(jnp.int32, (head_dim, page), 1) == 0
    for seq in range(n_seq):
        for h in range(n_heads):
            s_new = _sublane_allsum(_tile_sum(qb_ref[seq, h] * knb_ref[seq, h]))
            md_sc[seq, h] = jnp.where(first, s_new, NEG_BIG)
            ld_sc[seq, h] = jnp.where(first, 1.0, 0.0)
            accd_sc[seq, h] = jnp.where(first_d, vnb_ref[seq, h], 0.0)
    rd_sc[...] = lfn_ref[...]

    m_sc[...] = jnp.full_like(m_sc, -jnp.inf)
    acc_sc[...] = jnp.zeros_like(acc_sc)

    def scores(i, j, st_ref):
        off = pl.multiple_of(j * tk, tk)
        for hh in range(2):
            st_ref[hh] = _dot(k_ref[0, hh, pl.ds(off, tk), :], qt_ref[0, hh, i])

    def update(i, j, st_ref, masked):
        for hh in range(2):
            st = st_ref[hh]
            if masked:
                kpos = lax.broadcasted_iota(jnp.int32, st.shape, 0)
                qpos = lax.broadcasted_iota(jnp.int32, st.shape, 1)
                st = jnp.where(kpos <= qpos, st, -jnp.inf)
            m_old = m_sc[i, hh]
            m_new = jnp.maximum(m_old, jnp.max(st, axis=0, keepdims=True))
            p = jnp.exp2(st - m_new)
            acc_sc[i, hh] = jnp.exp2(m_old - m_new) * acc_sc[i, hh] + _dot(vt_ref[0, hh, j], p.astype(BF16))
            m_sc[i, hh] = m_new

    scores(oi_ref[0], oj_ref[0], sta_sc)

    def below(t, carry):
        slot = lax.rem(t, 2)
        p0 = t * pages_below
        wait_pages(p0, pages_below, slot)

        @pl.when(t + 1 < trips_below)
        def _():
            start_pages(p0 + pages_below, pages_below, 1 - slot)

        @pl.when(t + 1 == trips_below)
        def _():
            start_pages(p0 + pages_below, pages_diag, 1 - slot)

        decode_pages(p0, pages_below, slot)
        for u in range(unroll):
            n = unroll * t + u
            scores(oi_ref[n + 1], oj_ref[n + 1], bufs[(u + 1) % 2])
            update(oi_ref[n], oj_ref[n], bufs[u % 2], False)
        return carry

    lax.fori_loop(0, trips_below, below, 0)

    def finish(i):
        ot = jnp.concatenate([acc_sc[i, hh, 0:head_dim] / acc_sc[i, hh, head_dim:head_dim + 1] for hh in range(2)],
                             axis=0)
        o_ref[0, pl.ds(pl.multiple_of(i * tq, tq), tq), :] = ot.T

    def diagonal(t, carry):
        slot = lax.rem(trips_below + t, 2)
        p0 = trips_below * pages_below + t * pages_diag
        wait_pages(p0, pages_diag, slot)

        @pl.when(t + 1 < trips_diag)
        def _():
            start_pages(p0 + pages_diag, pages_diag, 1 - slot)

        decode_pages(p0, pages_diag, slot)
        for u in range(2):
            i = 2 * t + u
            nxt = jnp.minimum(i + 1, n_q - 1)
            scores(nxt, nxt, bufs[(u + 1) % 2])
            update(i, i, bufs[u % 2], True)
            finish(i)
        return carry

    lax.fori_loop(0, trips_diag, diagonal, 0)

    for seq in range(n_seq):
        for h in range(n_heads):
            m = md_sc[seq, h]
            w = jnp.exp(m - jnp.max(m, axis=-1, keepdims=True))
            l_tot = jnp.sum(ld_sc[seq, h] * w, axis=-1, keepdims=True)
            o = _lane_sum_to_row(_scale_cols(accd_sc[seq, h], w), ones_ref[...])
            od_ref[seq, h:h + 1, :] = o / l_tot[0:1]


def _fox(qaugt, kaug, vtb, page_table, qb, knb, vnb, lfn, cache_kt, cache_vt, cache_lft, *, head_dim):
    B, H, n_q, _, tq = qaugt.shape
    _, _, nt, v_rows, tk = vtb.shape
    S = n_q * tq
    Bd, n_pages = page_table.shape
    _, n_heads, _, page = cache_kt.shape
    assert tk == tq and nt == n_q and n_q % 2 == 0
    below = [(i, j) for i in range(n_q) for j in range(i)]
    unroll = 4 if len(below) % 4 == 0 else 2
    assert len(below) % unroll == 0
    sched = np.asarray(below + [(0, 0)], np.int32)

    n_steps = B * (H // 2)
    n_seq = Bd // n_steps
    trips_below, trips_diag = len(below) // unroll, n_q // 2
    pages_diag = 4
    pages_below = (n_seq * n_pages - trips_diag * pages_diag) // trips_below
    assert Bd % n_steps == 0 and trips_below * pages_below + trips_diag * pages_diag == n_seq * n_pages
    assert n_pages % pages_below == 0 and n_pages % pages_diag == 0 and (trips_below * pages_below) % pages_diag == 0

    scan = jnp.asarray(np.concatenate([np.tril(np.ones((page, page), np.float32), -1),
                                       np.ones((page, page), np.float32)], axis=1), BF16)
    ones = jnp.ones((2 * SUBLANES, page), BF16)
    once = pl.Buffered(1)
    pair = lambda shape: pl.BlockSpec((1, 2) + shape, lambda b, hp, *_: (b, hp) + (0,) * len(shape),
                                      pipeline_mode=once)
    seqs = lambda a: pl.BlockSpec((n_seq,) + a.shape[1:],
                                  lambda b, hp, *_: (b * (H // 2) + hp,) + (0,) * (a.ndim - 1), pipeline_mode=once)
    const = lambda a: pl.BlockSpec(a.shape, lambda b, hp, *_: (0,) * a.ndim, pipeline_mode=once)
    hbm = pl.BlockSpec(memory_space=pl.ANY)
    grid_spec = pltpu.PrefetchScalarGridSpec(
        num_scalar_prefetch=3,
        grid=(B, H // 2),
        in_specs=[pair((n_q, LANES, tq)), pair((S, LANES)), pair((nt, v_rows, tk)),
                  seqs(qb), seqs(knb), seqs(vnb), seqs(lfn), const(scan), const(ones), hbm, hbm, hbm],
        out_specs=[pl.BlockSpec((1, S, 2 * head_dim), lambda b, hp, *_: (b, 0, hp)),
                   pl.BlockSpec((n_seq, n_heads, head_dim), lambda b, hp, *_: (b * (H // 2) + hp, 0, 0))],
        scratch_shapes=[pltpu.VMEM((2, tk, tq), F32), pltpu.VMEM((2, tk, tq), F32),
                        pltpu.VMEM((n_q, 2, 1, tq), F32), pltpu.VMEM((n_q, 2, v_rows, tq), F32),
                        pltpu.VMEM((2, pages_below, n_heads, head_dim, page), F32),
                        pltpu.VMEM((2, pages_below, n_heads, head_dim, page), F32),
                        pltpu.VMEM((2, pages_below, n_heads, page), F32),
                        pltpu.SemaphoreType.DMA((2,)), pltpu.SemaphoreType.DMA((2,)), pltpu.SemaphoreType.DMA((2,)),
                        pltpu.VMEM((n_seq, n_heads, SUBLANES, page), F32),
                        pltpu.VMEM((n_seq, n_heads, SUBLANES, page), F32),
                        pltpu.VMEM((n_seq, n_heads, page), F32),
                        pltpu.VMEM((n_seq, n_heads, head_dim, page), F32)],
    )
    return pl.pallas_call(
        functools.partial(_fox_kernel, head_dim=head_dim, n_below=len(below), unroll=unroll, n_pages=n_pages,
                          pages_below=pages_below, pages_diag=pages_diag),
        grid_spec=grid_spec,
        out_shape=[jax.ShapeDtypeStruct((B, S, H * head_dim), F32),
                   jax.ShapeDtypeStruct((Bd, n_heads, head_dim), F32)],
        compiler_params=pltpu.CompilerParams(dimension_semantics=("arbitrary", "arbitrary"),
                                             vmem_limit_bytes=VMEM_LIMIT),
    )(jnp.asarray(sched[:, 0]), jnp.asarray(sched[:, 1]), page_table.reshape(-1),
      qaugt, kaug, vtb, qb, knb, vnb, lfn, scan, ones, cache_kt, cache_vt, cache_lft)


def _mem_attn_sample_kernel(qb_ref, mk_ref, mv_ref, ones_ref, o_ref):
    n_heads = qb_ref.shape[1]
    for h in range(n_heads):
        s = _sublane_allsum(_tile_sum(mk_ref[0, h] * qb_ref[0, h]))
        p = jnp.exp(s - jnp.max(s, axis=-1, keepdims=True))
        l = jnp.sum(p, axis=-1, keepdims=True)
        o = _lane_sum_to_row(_scale_cols(mv_ref[0, h], p), ones_ref[...])
        o_ref[0, h:h + 1, :] = o / l[0:1]


def _mem_attn_sample(qb, mkt, mvt):
    Bd, n_heads, head_dim, m_len = mkt.shape
    ones = jnp.ones((2 * SUBLANES, m_len), BF16)
    per_b = pl.BlockSpec((1, n_heads, head_dim, m_len), lambda b: (b, 0, 0, 0))
    return pl.pallas_call(
        _mem_attn_sample_kernel,
        grid=(Bd,),
        in_specs=[per_b, per_b, per_b, pl.BlockSpec(ones.shape, lambda b: (0, 0))],
        out_specs=pl.BlockSpec((1, n_heads, head_dim), lambda b: (b, 0, 0)),
        out_shape=jax.ShapeDtypeStruct((Bd, n_heads, head_dim), F32),
        compiler_params=pltpu.CompilerParams(dimension_semantics=("arbitrary",)),
    )(qb, mkt, mvt, ones)


def _lane_broadcast(x, n):
    return jnp.broadcast_to(x[..., None], x.shape + (n,))


def kernel(x_prompt, x_sample, mem_prompt, cache_k, cache_v, cache_logf, state_conv, cache_mem_k, cache_mem_v,
           page_table, norm1_g, w_in, b_f, conv_w, mem_norm_g, w_mem_kv, out_norm_g, w_out, norm2_g,
           w_gate, w_up, w_down, final_norm_g):
    B, S, D = x_prompt.shape
    Bd, Td, _ = x_sample.shape
    depth, n_pool, page, n_heads, head_dim = cache_k.shape
    mem_heads = cache_mem_k.shape[3]
    mem_len = cache_mem_k.shape[2]
    conv_k, cw_width = conv_w.shape[1], conv_w.shape[2]
    fox_width = n_heads * head_dim
    mem_width = mem_heads * head_dim
    assert depth == 1, "one layer: the fused tail applies the final norm"
    assert Td == 1 and conv_k == 3 and head_dim * 2 == LANES and n_heads % 2 == 0 and mem_heads % 2 == 0
    assert S % ROW_TILE == 0 and 3 * n_heads < LANES
    assert page % LANES == 0 and mem_len % LANES == 0

    selqt, selk = _sel_matrices(n_heads, head_dim)
    row2 = lambda a: a.reshape(1, -1)
    xs = x_sample.reshape(Bd, D)

    offs = np.cumsum([0, cw_width, cw_width, cw_width, fox_width, fox_width, fox_width, n_heads, mem_width])
    wl = w_in[0]
    w_f = jnp.pad(wl[:, offs[6]:offs[7]], ((0, 0), (0, LANES - n_heads)))
    w_cat = jnp.concatenate([wl[:, :offs[6]], wl[:, offs[7]:offs[8]], w_f], axis=1).astype(BF16)
    w_nat = jnp.concatenate([wl[:, :offs[3]], wl[:, offs[4]:offs[5]], wl[:, offs[7]:offs[8]], w_f],
                            axis=1).astype(BF16)
    w_t = jnp.concatenate([wl[:, offs[3]:offs[4]], wl[:, offs[5]:offs[6]]], axis=1).T.astype(BF16)
    bf_pad = jnp.pad(b_f[0], (0, LANES - n_heads)).reshape(1, LANES)
    g1 = row2(norm1_g[0])
    post_args = (row2(out_norm_g[0]), w_out[0].astype(BF16), row2(norm2_g[0]), w_gate[0].astype(BF16),
                 w_up[0].astype(BF16), w_down[0].astype(BF16), row2(final_norm_g))

    yconv, qaugt, kaug, k_p, vt_p, vtb, logf_p, qm, utail = _inproj_prompt(
        x_prompt, g1, w_nat, w_t, bf_pad, conv_w[0], selqt, selk,
        n_heads=n_heads, head_dim=head_dim, mem_width=mem_width)
    v_p = jnp.transpose(vt_p.reshape(B, n_heads, head_dim, S), (0, 3, 1, 2))
    mk, mv = _mem_kv(mem_prompt, row2(mem_norm_g[0]), w_mem_kv[0].astype(BF16))
    ymem = _mem_attn_prompt(qm, mk, mv)

    yconv_s, u_s, q_s, k_s, v_s, logf_s, qm_s = _inproj_sample(
        xs, g1, w_cat, bf_pad, conv_w[0], state_conv[0, :, 0], state_conv[0, :, 1],
        n_heads=n_heads, head_dim=head_dim, mem_width=mem_width)
    per_head = lambda a, nh: a.reshape(Bd, nh, head_dim)

    yfox, yfox_s = _fox(qaugt, kaug, vtb, page_table,
                        _lane_broadcast(per_head(q_s, n_heads), page), _lane_broadcast(per_head(k_s, n_heads), page),
                        _lane_broadcast(per_head(v_s, n_heads), page), _lane_broadcast(logf_s[:, :n_heads], page),
                        jnp.transpose(cache_k[0], (0, 2, 3, 1)), jnp.transpose(cache_v[0], (0, 2, 3, 1)),
                        jnp.transpose(cache_logf[0], (0, 2, 1)), head_dim=head_dim)
    y_prompt = _post(x_prompt.reshape(B * S, D), yconv.reshape(B * S, cw_width), yfox.reshape(B * S, fox_width),
                     ymem.reshape(B * S, mem_width), *post_args, tm=ROW_TILE)
    ymem_s = _mem_attn_sample(_lane_broadcast(per_head(qm_s, mem_heads), mem_len),
                              jnp.transpose(cache_mem_k[0], (0, 2, 3, 1)), jnp.transpose(cache_mem_v[0], (0, 2, 3, 1)))
    y_sample = _post(xs, yconv_s, yfox_s.reshape(Bd, fox_width), ymem_s.reshape(Bd, mem_width), *post_args, tm=Bd)

    return (y_prompt.reshape(B, S, D), y_sample.reshape(Bd, 1, D),
            k_p.reshape(1, B, S, n_heads, head_dim), v_p.reshape(1, B, S, n_heads, head_dim), logf_p[None],
            utail[None, :, SUBLANES - (conv_k - 1):],
            mk.reshape(1, B, mem_len, mem_heads, head_dim), mv.reshape(1, B, mem_len, mem_heads, head_dim),
            k_s.reshape(1, Bd, 1, n_heads, head_dim), v_s.reshape(1, Bd, 1, n_heads, head_dim),
            logf_s[None, :, None, :n_heads], jnp.stack([state_conv[0, :, 1], u_s], axis=1)[None])
```

```python
import functools

import numpy as np
import jax
import jax.numpy as jnp
from jax import lax
from jax.experimental import pallas as pl
from jax.experimental.pallas import tpu as pltpu

F32 = jnp.float32
BF16 = jnp.bfloat16
EPS = 1e-6
LOG2E = 1.4426950408889634

LANES = 128
SUBLANES = 8
ROW_TILE = 512
MXU_WIDTH = 256
VMEM_LIMIT = 56 * 1024 * 1024


def _dot(a, b):
    return jnp.dot(a, b, preferred_element_type=F32)


def _dot_nt(a, b):
    return lax.dot_general(a, b, (((1,), (1,)), ((), ())), preferred_element_type=F32)


def _normalize(x):
    return x * lax.rsqrt(jnp.mean(x * x, axis=-1, keepdims=True) + EPS)


def _split3(x):
    hi = x.astype(BF16)
    r = x - hi.astype(F32)
    mid = r.astype(BF16)
    lo = (r - mid.astype(F32)).astype(BF16)
    return hi, mid, lo


def _dot_exact_lhs(x, m):
    hi, mid, lo = _split3(x)
    return _dot(hi, m) + _dot(mid, m) + _dot(lo, m)


def _log_sigmoid(x):
    return jnp.minimum(x, 0.0) - jnp.log1p(jnp.exp(-jnp.abs(x)))


def _inproj_prompt_kernel(x_ref, g_ref, w_ref, wt_ref, bf_ref, cw_ref, selqt_ref, selk_ref,
                          yconv_ref, qaugt_ref, kaug_ref, k_ref, vt_ref, vtb_ref, logf_ref, qm_ref, utail_ref,
                          ucarry, ccarry, *, cw_width, fox_width, mem_width, n_heads, scale):
    t = pl.program_id(1)
    tm = x_ref.shape[1]
    head_dim = fox_width // n_heads

    @pl.when(t == 0)
    def _():
        ucarry[...] = jnp.zeros_like(ucarry)
        ccarry[...] = jnp.zeros_like(ccarry)

    x = x_ref[0]
    xn = (_normalize(x) * g_ref[...]).astype(BF16)

    o_k = 3 * cw_width
    o_m = o_k + fox_width
    o_f = o_m + mem_width

    pc = _dot(xn, w_ref[:, 0:o_k])
    b_c = pc[:, 0:cw_width]
    u = pc[:, cw_width:2 * cw_width] * pc[:, 2 * cw_width:3 * cw_width]
    prev = ucarry[...]
    row8 = lax.broadcasted_iota(jnp.int32, (SUBLANES, cw_width), 0)
    r1 = pltpu.roll(u, 1, 0)
    r2 = pltpu.roll(u, 2, 0)
    h1 = jnp.where(row8 < 1, pltpu.roll(prev, 1, 0), r1[0:SUBLANES])
    h2 = jnp.where(row8 < 2, pltpu.roll(prev, 2, 0), r2[0:SUBLANES])
    u1 = jnp.concatenate([h1, r1[SUBLANES:]], axis=0)
    u2 = jnp.concatenate([h2, r2[SUBLANES:]], axis=0)
    cw = cw_ref[...]
    yconv_ref[0] = b_c * (cw[0:1] * u2 + cw[1:2] * u1 + cw[2:3] * u)
    tail = u[tm - SUBLANES:]
    ucarry[...] = tail
    utail_ref[0] = tail

    pf = _dot(xn, w_ref[:, o_m:o_f + LANES])
    qm_ref[0] = (pf[:, 0:mem_width] * scale).astype(BF16)
    logf = _log_sigmoid(pf[:, mem_width:mem_width + LANES] + bf_ref[...])
    logf_ref[0] = logf[:, 0:n_heads]

    lane = lax.broadcasted_iota(jnp.int32, (tm, LANES), 1)
    row = lax.broadcasted_iota(jnp.int32, (tm, LANES), 0)
    c = jnp.where(lane < n_heads, logf, 0.0)
    sh = 1
    while sh < tm:
        c = c + jnp.where(row >= sh, pltpu.roll(c, sh, 0), 0.0)
        sh *= 2
    c = c + ccarry[0:1]
    ccarry[...] = jnp.broadcast_to(c[tm - 1:tm], ccarry.shape)

    cs = c * LOG2E
    hi = cs.astype(BF16).astype(F32)
    r = cs - hi
    mid = r.astype(BF16).astype(F32)
    lo = r - mid
    parts = (hi + pltpu.roll(mid, n_heads, 1) + pltpu.roll(lo, 2 * n_heads, 1)
             + jnp.where(lane == 3 * n_heads, 1.0, 0.0)).astype(BF16)

    ek = _dot(parts, selk_ref[...])
    pk = _dot(xn, w_ref[:, o_k:o_m])
    k_ref[0] = pk
    half = lane < (LANES // 2)
    for hp in range(n_heads // 2):
        pair = pk[:, hp * LANES:(hp + 1) * LANES]
        rot = pltpu.roll(pair, LANES // 2, 1)
        e0 = ek[:, (2 * hp) * LANES:(2 * hp + 1) * LANES]
        e1 = ek[:, (2 * hp + 1) * LANES:(2 * hp + 2) * LANES]
        kaug_ref[0, 2 * hp] = jnp.where(half, pair, e0).astype(BF16)
        kaug_ref[0, 2 * hp + 1] = jnp.where(half, rot, e1).astype(BF16)

    qt = _dot_nt(wt_ref[0:fox_width, :], xn) * (scale * LOG2E)
    vt = _dot_nt(wt_ref[fox_width:2 * fox_width, :], xn)
    vt_ref[0] = vt
    ext = _dot_nt(selqt_ref[...], parts)
    pad = jnp.zeros((LANES - head_dim - SUBLANES, tm), F32)
    ones_rows = jnp.where(lax.broadcasted_iota(jnp.int32, (2 * SUBLANES, tm), 0) == 0, 1.0, 0.0)
    for h in range(n_heads):
        vtb_ref[0, h, 0] = jnp.concatenate([vt[h * head_dim:(h + 1) * head_dim], ones_rows], axis=0).astype(BF16)
        qaugt_ref[0, h, 0] = jnp.concatenate(
            [qt[h * head_dim:(h + 1) * head_dim], ext[h * SUBLANES:(h + 1) * SUBLANES], pad], axis=0).astype(BF16)


def _sel_matrices(n_heads, head_dim):
    selqt = np.zeros((n_heads * SUBLANES, LANES), np.float32)
    selk = np.zeros((LANES, n_heads * LANES), np.float32)
    for h in range(n_heads):
        base = h * LANES + head_dim
        for part in range(3):
            selqt[h * SUBLANES + part, part * n_heads + h] = 1.0
            selqt[h * SUBLANES + 3 + part, 3 * n_heads] = 1.0
            selk[3 * n_heads, base + part] = 1.0
            selk[part * n_heads + h, base + 3 + part] = -1.0
    return jnp.asarray(selqt, BF16), jnp.asarray(selk, BF16)


def _inproj_prompt(x, g, w, wt, bf, cw, selqt, selk, *, n_heads, head_dim, mem_width):
    B, S, D = x.shape
    cw_width = cw.shape[1]
    fox_width = n_heads * head_dim
    tm = ROW_TILE
    nt = S // tm
    kern = functools.partial(_inproj_prompt_kernel, cw_width=cw_width, fox_width=fox_width,
                             mem_width=mem_width, n_heads=n_heads, scale=head_dim ** -0.5)
    const = lambda a: pl.BlockSpec(a.shape, lambda b, t: (0,) * a.ndim)
    row_spec = lambda width: pl.BlockSpec((1, tm, width), lambda b, t: (b, t, 0))
    return pl.pallas_call(
        kern,
        grid=(B, nt),
        in_specs=[row_spec(D), const(g), const(w), const(wt), const(bf), const(cw), const(selqt), const(selk)],
        out_specs=[row_spec(cw_width),
                   pl.BlockSpec((1, n_heads, 1, LANES, tm), lambda b, t: (b, 0, t, 0, 0)),
                   pl.BlockSpec((1, n_heads, tm, LANES), lambda b, t: (b, 0, t, 0)),
                   row_spec(fox_width),
                   pl.BlockSpec((1, fox_width, tm), lambda b, t: (b, 0, t)),
                   pl.BlockSpec((1, n_heads, 1, head_dim + 2 * SUBLANES, tm), lambda b, t: (b, 0, t, 0, 0)),
                   row_spec(n_heads), row_spec(mem_width),
                   pl.BlockSpec((1, SUBLANES, cw_width), lambda b, t: (b, 0, 0))],
        out_shape=[jax.ShapeDtypeStruct((B, S, cw_width), F32),
                   jax.ShapeDtypeStruct((B, n_heads, nt, LANES, tm), BF16),
                   jax.ShapeDtypeStruct((B, n_heads, S, LANES), BF16),
                   jax.ShapeDtypeStruct((B, S, fox_width), F32),
                   jax.ShapeDtypeStruct((B, fox_width, S), F32),
                   jax.ShapeDtypeStruct((B, n_heads, nt, head_dim + 2 * SUBLANES, tm), BF16),
                   jax.ShapeDtypeStruct((B, S, n_heads), F32),
                   jax.ShapeDtypeStruct((B, S, mem_width), BF16),
                   jax.ShapeDtypeStruct((B, SUBLANES, cw_width), F32)],
        scratch_shapes=[pltpu.VMEM((SUBLANES, cw_width), F32), pltpu.VMEM((SUBLANES, LANES), F32)],
        compiler_params=pltpu.CompilerParams(dimension_semantics=("arbitrary", "arbitrary"),
                                             vmem_limit_bytes=VMEM_LIMIT),
    )(x, g, w, wt, bf, cw, selqt, selk)


def _mem_kv_kernel(mem_ref, g_ref, w_ref, mk_ref, mv_ref):
    h = (_normalize(mem_ref[0]) * g_ref[...]).astype(BF16)
    kv = _dot(h, w_ref[...])
    width = mk_ref.shape[2]
    mk_ref[0] = kv[:, 0:width]
    mv_ref[0] = kv[:, width:2 * width]


def _mem_kv(mem, g, w):
    B, M, D = mem.shape
    width = w.shape[1] // 2
    out = jax.ShapeDtypeStruct((B, M, width), F32)
    return pl.pallas_call(
        _mem_kv_kernel,
        grid=(B,),
        in_specs=[pl.BlockSpec((1, M, D), lambda b: (b, 0, 0)),
                  pl.BlockSpec(g.shape, lambda b: (0, 0)),
                  pl.BlockSpec(w.shape, lambda b: (0, 0))],
        out_specs=[pl.BlockSpec((1, M, width), lambda b: (b, 0, 0))] * 2,
        out_shape=[out, out],
        compiler_params=pltpu.CompilerParams(dimension_semantics=("arbitrary",)),
    )(mem, g, w)


def _mem_attention_rows(q, mk, mv):
    rows, width = q.shape
    lane_k = lax.broadcasted_iota(jnp.int32, (mk.shape[0], LANES), 1)
    lane_o = lax.broadcasted_iota(jnp.int32, (rows, LANES), 1)
    tiles = []
    for hp in range(width // LANES):
        sl = slice(hp * LANES, (hp + 1) * LANES)
        mk_pair = mk[:, sl]
        mv_pair = mv[:, sl].astype(BF16)
        outs = []
        for hh in range(2):
            in_head = (lane_k < LANES // 2) if hh == 0 else (lane_k >= LANES // 2)
            s = _dot_nt(q[:, sl], jnp.where(in_head, mk_pair, 0.0).astype(BF16))
            p = jnp.exp(s - jnp.max(s, axis=-1, keepdims=True))
            l = jnp.sum(p, axis=-1, keepdims=True)
            outs.append(_dot(p.astype(BF16), mv_pair) / l)
        tiles.append(jnp.where(lane_o < LANES // 2, outs[0], outs[1]))
    return jnp.concatenate(tiles, axis=1)


def _post_kernel(x_ref, yc_ref, yf_ref, ym_ref, *rest, ff_split):
    _post_body(x_ref, yc_ref, yf_ref, ym_ref[...], *rest, ff_split=ff_split)


def _post_mem_kernel(x_ref, yc_ref, yf_ref, qm_ref, mk_ref, mv_ref, *rest, ff_split):
    ym = _mem_attention_rows(qm_ref[...], mk_ref[0], mv_ref[0])
    _post_body(x_ref, yc_ref, yf_ref, ym, *rest, ff_split=ff_split)


def _post_body(x_ref, yc_ref, yf_ref, ym, og_ref, wo_ref, n2_ref, wg_ref, wu_ref, wd_ref, fg_ref, o_ref, *, ff_split):
    og = og_ref[...]
    wc = yc_ref.shape[1]
    wf = yf_ref.shape[1]
    wm = ym.shape[1]
    a = (_normalize(yc_ref[...]) * og[:, 0:wc]).astype(BF16)
    b = (_normalize(yf_ref[...]) * og[:, wc:wc + wf]).astype(BF16)
    c = (_normalize(ym) * og[:, wc + wf:wc + wf + wm]).astype(BF16)
    mix = (_dot(a, wo_ref[0:wc, :]) + _dot(b, wo_ref[wc:wc + wf, :])
           + _dot(c, wo_ref[wc + wf:wc + wf + wm, :]))
    x1 = x_ref[...] + mix
    h = (_normalize(x1) * n2_ref[...]).astype(BF16)
    d_ff = wg_ref.shape[1]
    ffn = jnp.zeros_like(x1)
    for sl in (slice(0, ff_split), slice(ff_split, d_ff)):
        gate = _dot(h, wg_ref[:, sl])
        up = _dot(h, wu_ref[:, sl])
        act = (gate * jax.nn.sigmoid(gate) * up).astype(BF16)
        ffn = ffn + _dot(act, wd_ref[sl, :])
    x2 = x1 + ffn
    o_ref[...] = _normalize(x2) * fg_ref[...]


def _post(x, yc, yf, ym, og, wo, n2, wg, wu, wd, fg, *, tm, mem_kv=None):
    N, D = x.shape
    d_ff = wg.shape[1]
    ff_split = pl.cdiv(d_ff // 2, MXU_WIDTH) * MXU_WIDTH
    const = lambda a: pl.BlockSpec(a.shape, lambda t: (0, 0), pipeline_mode=pl.Buffered(1))
    rows = lambda a: pl.BlockSpec((tm, a.shape[1]), lambda t: (t, 0))
    weights = (og, wo, n2, wg, wu, wd, fg)
    if mem_kv is None:
        kern, mem_specs = _post_kernel, []
    else:
        tiles_per_batch = N // mem_kv[0].shape[0] // tm
        kern = _post_mem_kernel
        mem_specs = [pl.BlockSpec((1,) + a.shape[1:], lambda t: (t // tiles_per_batch, 0, 0)) for a in mem_kv]
    return pl.pallas_call(
        functools.partial(kern, ff_split=ff_split),
        grid=(N // tm,),
        in_specs=[rows(x), rows(yc), rows(yf), rows(ym)] + mem_specs + [const(a) for a in weights],
        out_specs=pl.BlockSpec((tm, D), lambda t: (t, 0)),
        out_shape=jax.ShapeDtypeStruct((N, D), F32),
        compiler_params=pltpu.CompilerParams(dimension_semantics=("arbitrary",), vmem_limit_bytes=VMEM_LIMIT),
    )(x, yc, yf, ym, *(mem_kv or ()), *weights)


def _inproj_sample_kernel(x_ref, g_ref, w_ref, bf_ref, cw_ref, h0_ref, h1_ref,
                          yconv_ref, u_ref, q_ref, k_ref, v_ref, logf_ref, qm_ref,
                          *, cw_width, fox_width, mem_width, scale):
    xn = (_normalize(x_ref[...]) * g_ref[...]).astype(BF16)
    p = _dot(xn, w_ref[...])
    o_q = 3 * cw_width
    o_k = o_q + fox_width
    o_v = o_k + fox_width
    o_m = o_v + fox_width
    o_f = o_m + mem_width
    b_c = p[:, 0:cw_width]
    u = p[:, cw_width:2 * cw_width] * p[:, 2 * cw_width:3 * cw_width]
    cw = cw_ref[...]
    yconv_ref[...] = b_c * (cw[0:1] * h0_ref[...] + cw[1:2] * h1_ref[...] + cw[2:3] * u)
    u_ref[...] = u
    q_ref[...] = p[:, o_q:o_k] * scale
    k_ref[...] = p[:, o_k:o_v]
    v_ref[...] = p[:, o_v:o_m]
    qm_ref[...] = p[:, o_m:o_f] * scale
    logf_ref[...] = _log_sigmoid(p[:, o_f:o_f + LANES] + bf_ref[...])


def _inproj_sample(x, g, w, bf, cw, h0, h1, *, n_heads, head_dim, mem_width):
    N, D = x.shape
    cw_width = cw.shape[1]
    fox_width = n_heads * head_dim
    kern = functools.partial(_inproj_sample_kernel, cw_width=cw_width, fox_width=fox_width,
                             mem_width=mem_width, scale=head_dim ** -0.5)
    sds = lambda width: jax.ShapeDtypeStruct((N, width), F32)
    return pl.pallas_call(
        kern,
        out_shape=[sds(cw_width), sds(cw_width), sds(fox_width), sds(fox_width), sds(fox_width), sds(LANES),
                   sds(mem_width)],
        compiler_params=pltpu.CompilerParams(vmem_limit_bytes=VMEM_LIMIT),
    )(x, g, w, bf, cw, h0, h1)


NEG_BIG = -1e30


def _tile_sum(x):
    return jnp.sum(x.reshape(x.shape[0] // SUBLANES, SUBLANES, x.shape[1]), axis=0)


def _sublane_allsum(x):
    x = x + pltpu.roll(x, 4, 0)
    x = x + pltpu.roll(x, 2, 0)
    return x + pltpu.roll(x, 1, 0)


def _scale_cols(x, w):
    r, l = x.shape
    return (x.reshape(r // SUBLANES, SUBLANES, l) * w[None]).reshape(r, l)


def _lane_sum_to_row(x, ones):
    hi, mid, lo = _split3(x)
    return (_dot_nt(ones, hi) + _dot_nt(ones, mid) + _dot_nt(ones, lo))[0:1]


def _fox_kernel(oi_ref, oj_ref, pt_ref,
                qt_ref, k_ref, vt_ref, qb_ref, knb_ref, vnb_ref, lfn_ref, scan_ref, ones_ref,
                ck_hbm, cv_hbm, clf_hbm,
                o_ref, od_ref,
                sta_sc, stb_sc, m_sc, acc_sc,
                kbuf, vbuf, lfbuf, ksem, vsem, lfsem, md_sc, ld_sc, rd_sc, accd_sc,
                *, head_dim, n_below, unroll, n_pages, pages_below, pages_diag):
    n_q, tq = qt_ref.shape[2], qt_ref.shape[4]
    tk = tq
    n_seq, n_heads, _, page = qb_ref.shape
    trips_below = n_below // unroll
    trips_diag = n_q // 2
    step = pl.program_id(0) * pl.num_programs(1) + pl.program_id(1)
    n_steps = pl.num_programs(0) * pl.num_programs(1)
    bufs = (sta_sc, stb_sc)

    def page_copies(p0, count, slot, of_step=step):
        seq = of_step * n_seq + p0 // n_pages
        newest = seq * n_pages + (n_pages - 1) - lax.rem(p0, n_pages)
        copies = []
        for u in range(count):
            pool_page = pt_ref[newest - u]
            copies += [pltpu.make_async_copy(ck_hbm.at[pool_page], kbuf.at[slot, u], ksem.at[slot]),
                       pltpu.make_async_copy(cv_hbm.at[pool_page], vbuf.at[slot, u], vsem.at[slot]),
                       pltpu.make_async_copy(clf_hbm.at[pool_page], lfbuf.at[slot, u], lfsem.at[slot])]
        return copies

    def start_pages(p0, count, slot, of_step=step):
        for n, c in enumerate(page_copies(p0, count, slot, of_step)):
            c.start(priority=(n // 3) % 2)

    def wait_pages(p0, count, slot):
        for c in page_copies(p0, count, slot):
            c.wait()

    def decode_pages(p0, count, slot, first=0):
        seq = p0 // n_pages
        lf_all = jnp.concatenate([lfbuf[slot, first + u] for u in range(count)], axis=0)
        scans = _dot_exact_lhs(lf_all, scan_ref[...])
        r = rd_sc[seq]
        biases = []
        for u in range(count):
            rows = slice(u * n_heads, (u + 1) * n_heads)
            biases.append(r + scans[rows, 0:page])
            r = r + scans[rows, page:2 * page]
        rd_sc[seq] = r
        for h in range(n_heads):
            qb = qb_ref[seq, h]
            ss = []
            for u in range(count):
                s = _sublane_allsum(_tile_sum(kbuf[slot, first + u, h] * qb))
                ss.append(s + jnp.broadcast_to(biases[u][h:h + 1], s.shape))
            m_old = md_sc[seq, h]
            m_new = functools.reduce(jnp.maximum, ss, m_old)
            alpha = jnp.exp(m_old - m_new)
            l = ld_sc[seq, h] * alpha
            acc = _scale_cols(accd_sc[seq, h], alpha)
            for u in range(count):
                p = jnp.exp(ss[u] - m_new)
                l = l + p
                acc = acc + _scale_cols(vbuf[slot, first + u, h], p)
            md_sc[seq, h] = m_new
            ld_sc[seq, h] = l
            accd_sc[seq, h] = acc

    @pl.when(step == 0)
    def _():
        start_pages(0, pages_below, 0)

    first = lax.broadcasted_iota(jnp.int32, (SUBLANES, page), 1) == 0
    first_d = lax.broadcasted_iota(jnp.int32, (head_dim, page), 1) == 0
    for seq in range(n_seq):
        for h in range(n_heads):
            s_new = _sublane_allsum(_tile_sum(qb_ref[seq, h] * knb_ref[seq, h]))
            md_sc[seq, h] = jnp.where(first, s_new, NEG_BIG)
            ld_sc[seq, h] = jnp.where(first, 1.0, 0.0)
            accd_sc[seq, h] = jnp.where(first_d, vnb_ref[seq, h], 0.0)
    rd_sc[...] = lfn_ref[...]

    m_sc[...] = jnp.full_like(m_sc, -jnp.inf)
    acc_sc[...] = jnp.zeros_like(acc_sc)

    def scores(i, j, st_ref):
        off = pl.multiple_of(j * tk, tk)
        for hh in range(2):
            st_ref[hh] = _dot(k_ref[0, hh, pl.ds(off, tk), :], qt_ref[0, hh, i])

    def update(i, j, st_ref, masked):
        for hh in range(2):
            st = st_ref[hh]
            if masked:
                kpos = lax.broadcasted_iota(jnp.int32, st.shape, 0)
                qpos = lax.broadcasted_iota(jnp.int32, st.shape, 1)
                st = jnp.where(kpos <= qpos, st, -jnp.inf)
            m_old = m_sc[i, hh]
            m_new = jnp.maximum(m_old, jnp.max(st, axis=0, keepdims=True))
            p = jnp.exp2(st - m_new)
            acc_sc[i, hh] = jnp.exp2(m_old - m_new) * acc_sc[i, hh] + _dot(vt_ref[0, hh, j], p.astype(BF16))
            m_sc[i, hh] = m_new

    scores(oi_ref[0], oj_ref[0], sta_sc)

    def below(t, carry):
        slot = lax.rem(t, 2)
        p0 = t * pages_below
        wait_pages(p0, pages_below, slot)

        @pl.when(t + 1 < trips_below)
        def _():
            start_pages(p0 + pages_below, pages_below, 1 - slot)

        @pl.when(t + 1 == trips_below)
        def _():
            start_pages(p0 + pages_below, pages_diag, 1 - slot)

        sub = pages_below // unroll
        for u in range(unroll):
            n = unroll * t + u
            decode_pages(p0 + u * sub, sub, slot, first=u * sub)
            scores(oi_ref[n + 1], oj_ref[n + 1], bufs[(u + 1) % 2])
            update(oi_ref[n], oj_ref[n], bufs[u % 2], False)
        return carry

    lax.fori_loop(0, trips_below, below, 0)

    def finish(i):
        ot = jnp.concatenate([acc_sc[i, hh, 0:head_dim] / acc_sc[i, hh, head_dim:head_dim + 1] for hh in range(2)],
                             axis=0)
        o_ref[0, pl.ds(pl.multiple_of(i * tq, tq), tq), :] = ot.T

    def diagonal(t, carry):
        slot = lax.rem(trips_below + t, 2)
        p0 = trips_below * pages_below + t * pages_diag
        wait_pages(p0, pages_diag, slot)

        @pl.when(t + 1 < trips_diag)
        def _():
            start_pages(p0 + pages_diag, pages_diag, 1 - slot)

        @pl.when(jnp.logical_and(t + 1 == trips_diag, step + 1 < n_steps))
        def _():
            start_pages(0, pages_below, 1 - slot, of_step=step + 1)

        decode_pages(p0, pages_diag, slot)
        for u in range(2):
            i = 2 * t + u
            nxt = jnp.minimum(i + 1, n_q - 1)
            scores(nxt, nxt, bufs[(u + 1) % 2])
            update(i, i, bufs[u % 2], True)
            finish(i)
        return carry

    lax.fori_loop(0, trips_diag, diagonal, 0)

    for seq in range(n_seq):
        for h in range(n_heads):
            m = md_sc[seq, h]
            w = jnp.exp(m - jnp.max(m, axis=-1, keepdims=True))
            l_tot = jnp.sum(ld_sc[seq, h] * w, axis=-1, keepdims=True)
            o = _lane_sum_to_row(_scale_cols(accd_sc[seq, h], w), ones_ref[...])
            od_ref[seq, h:h + 1, :] = o / l_tot[0:1]


def _fox(qaugt, kaug, vtb, page_table, qb, knb, vnb, lfn, cache_kt, cache_vt, cache_lft, *, head_dim):
    B, H, n_q, _, tq = qaugt.shape
    _, _, nt, v_rows, tk = vtb.shape
    S = n_q * tq
    Bd, n_pages = page_table.shape
    _, n_heads, _, page = cache_kt.shape
    assert tk == tq and nt == n_q and n_q % 2 == 0
    below = [(i, j) for i in range(n_q) for j in range(i)]
    unroll = 4 if len(below) % 4 == 0 else 2
    assert len(below) % unroll == 0
    sched = np.asarray(below + [(0, 0)], np.int32)

    n_steps = B * (H // 2)
    n_seq = Bd // n_steps
    trips_below, trips_diag = len(below) // unroll, n_q // 2
    pages_diag = 4
    pages_below = (n_seq * n_pages - trips_diag * pages_diag) // trips_below
    assert Bd % n_steps == 0 and trips_below * pages_below + trips_diag * pages_diag == n_seq * n_pages
    assert n_pages % pages_below == 0 and n_pages % pages_diag == 0 and (trips_below * pages_below) % pages_diag == 0
    assert (trips_below + trips_diag) % 2 == 0, "the two page buffers must be back at slot 0 when a grid step ends"

    scan = jnp.asarray(np.concatenate([np.tril(np.ones((page, page), np.float32), -1),
                                       np.ones((page, page), np.float32)], axis=1), BF16)
    ones = jnp.ones((2 * SUBLANES, page), BF16)
    once = pl.Buffered(1)
    pair = lambda shape: pl.BlockSpec((1, 2) + shape, lambda b, hp, *_: (b, hp) + (0,) * len(shape),
                                      pipeline_mode=once)
    seqs = lambda a: pl.BlockSpec((n_seq,) + a.shape[1:],
                                  lambda b, hp, *_: (b * (H // 2) + hp,) + (0,) * (a.ndim - 1), pipeline_mode=once)
    const = lambda a: pl.BlockSpec(a.shape, lambda b, hp, *_: (0,) * a.ndim, pipeline_mode=once)
    hbm = pl.BlockSpec(memory_space=pl.ANY)
    grid_spec = pltpu.PrefetchScalarGridSpec(
        num_scalar_prefetch=3,
        grid=(B, H // 2),
        in_specs=[pair((n_q, LANES, tq)), pair((S, LANES)), pair((nt, v_rows, tk)),
                  seqs(qb), seqs(knb), seqs(vnb), seqs(lfn), const(scan), const(ones), hbm, hbm, hbm],
        out_specs=[pl.BlockSpec((1, S, 2 * head_dim), lambda b, hp, *_: (b, 0, hp)),
                   pl.BlockSpec((n_seq, n_heads, head_dim), lambda b, hp, *_: (b * (H // 2) + hp, 0, 0))],
        scratch_shapes=[pltpu.VMEM((2, tk, tq), F32), pltpu.VMEM((2, tk, tq), F32),
                        pltpu.VMEM((n_q, 2, 1, tq), F32), pltpu.VMEM((n_q, 2, v_rows, tq), F32),
                        pltpu.VMEM((2, pages_below, n_heads, head_dim, page), F32),
                        pltpu.VMEM((2, pages_below, n_heads, head_dim, page), F32),
                        pltpu.VMEM((2, pages_below, n_heads, page), F32),
                        pltpu.SemaphoreType.DMA((2,)), pltpu.SemaphoreType.DMA((2,)), pltpu.SemaphoreType.DMA((2,)),
                        pltpu.VMEM((n_seq, n_heads, SUBLANES, page), F32),
                        pltpu.VMEM((n_seq, n_heads, SUBLANES, page), F32),
                        pltpu.VMEM((n_seq, n_heads, page), F32),
                        pltpu.VMEM((n_seq, n_heads, head_dim, page), F32)],
    )
    return pl.pallas_call(
        functools.partial(_fox_kernel, head_dim=head_dim, n_below=len(below), unroll=unroll, n_pages=n_pages,
                          pages_below=pages_below, pages_diag=pages_diag),
        grid_spec=grid_spec,
        out_shape=[jax.ShapeDtypeStruct((B, S, H * head_dim), F32),
                   jax.ShapeDtypeStruct((Bd, n_heads, head_dim), F32)],
        compiler_params=pltpu.CompilerParams(dimension_semantics=("arbitrary", "arbitrary"),
                                             vmem_limit_bytes=VMEM_LIMIT),
    )(jnp.asarray(sched[:, 0]), jnp.asarray(sched[:, 1]), page_table.reshape(-1),
      qaugt, kaug, vtb, qb, knb, vnb, lfn, scan, ones, cache_kt, cache_vt, cache_lft)


MEM_SEQS_PER_STEP = 4


def _mem_attn_sample_kernel(qb_ref, mk_ref, mv_ref, ones_ref, o_ref):
    n_seq, n_heads = qb_ref.shape[0], qb_ref.shape[1]
    for b in range(n_seq):
        for h in range(n_heads):
            s = _sublane_allsum(_tile_sum(mk_ref[b, h] * qb_ref[b, h]))
            p = jnp.exp(s - jnp.max(s, axis=-1, keepdims=True))
            l = jnp.sum(p, axis=-1, keepdims=True)
            o = _lane_sum_to_row(_scale_cols(mv_ref[b, h], p), ones_ref[...])
            o_ref[b, h:h + 1, :] = o / l[0:1]


def _mem_attn_sample(qb, mkt, mvt):
    Bd, n_heads, head_dim, m_len = mkt.shape
    n_seq = MEM_SEQS_PER_STEP
    assert Bd % n_seq == 0
    ones = jnp.ones((2 * SUBLANES, m_len), BF16)
    per_b = pl.BlockSpec((n_seq, n_heads, head_dim, m_len), lambda b: (b, 0, 0, 0))
    return pl.pallas_call(
        _mem_attn_sample_kernel,
        grid=(Bd // n_seq,),
        in_specs=[per_b, per_b, per_b, pl.BlockSpec(ones.shape, lambda b: (0, 0))],
        out_specs=pl.BlockSpec((n_seq, n_heads, head_dim), lambda b: (b, 0, 0)),
        out_shape=jax.ShapeDtypeStruct((Bd, n_heads, head_dim), F32),
        compiler_params=pltpu.CompilerParams(dimension_semantics=("arbitrary",)),
    )(qb, mkt, mvt, ones)


def _lane_broadcast(x, n):
    return jnp.broadcast_to(x[..., None], x.shape + (n,))


def kernel(x_prompt, x_sample, mem_prompt, cache_k, cache_v, cache_logf, state_conv, cache_mem_k, cache_mem_v,
           page_table, norm1_g, w_in, b_f, conv_w, mem_norm_g, w_mem_kv, out_norm_g, w_out, norm2_g,
           w_gate, w_up, w_down, final_norm_g):
    B, S, D = x_prompt.shape
    Bd, Td, _ = x_sample.shape
    depth, n_pool, page, n_heads, head_dim = cache_k.shape
    mem_heads = cache_mem_k.shape[3]
    mem_len = cache_mem_k.shape[2]
    conv_k, cw_width = conv_w.shape[1], conv_w.shape[2]
    fox_width = n_heads * head_dim
    mem_width = mem_heads * head_dim
    assert depth == 1, "one layer: the fused tail applies the final norm"
    assert Td == 1 and conv_k == 3 and head_dim * 2 == LANES and n_heads % 2 == 0 and mem_heads % 2 == 0
    assert S % ROW_TILE == 0 and 3 * n_heads < LANES
    assert page % LANES == 0 and mem_len % LANES == 0

    selqt, selk = _sel_matrices(n_heads, head_dim)
    row2 = lambda a: a.reshape(1, -1)
    xs = x_sample.reshape(Bd, D)

    offs = np.cumsum([0, cw_width, cw_width, cw_width, fox_width, fox_width, fox_width, n_heads, mem_width])
    wl = w_in[0]
    w_f = jnp.pad(wl[:, offs[6]:offs[7]], ((0, 0), (0, LANES - n_heads)))
    w_cat = jnp.concatenate([wl[:, :offs[6]], wl[:, offs[7]:offs[8]], w_f], axis=1).astype(BF16)
    w_nat = jnp.concatenate([wl[:, :offs[3]], wl[:, offs[4]:offs[5]], wl[:, offs[7]:offs[8]], w_f],
                            axis=1).astype(BF16)
    w_t = jnp.concatenate([wl[:, offs[3]:offs[4]], wl[:, offs[5]:offs[6]]], axis=1).T.astype(BF16)
    bf_pad = jnp.pad(b_f[0], (0, LANES - n_heads)).reshape(1, LANES)
    g1 = row2(norm1_g[0])
    post_args = (row2(out_norm_g[0]), w_out[0].astype(BF16), row2(norm2_g[0]), w_gate[0].astype(BF16),
                 w_up[0].astype(BF16), w_down[0].astype(BF16), row2(final_norm_g))

    yconv, qaugt, kaug, k_p, vt_p, vtb, logf_p, qm, utail = _inproj_prompt(
        x_prompt, g1, w_nat, w_t, bf_pad, conv_w[0], selqt, selk,
        n_heads=n_heads, head_dim=head_dim, mem_width=mem_width)
    v_p = jnp.transpose(vt_p.reshape(B, n_heads, head_dim, S), (0, 3, 1, 2))
    mk, mv = _mem_kv(mem_prompt, row2(mem_norm_g[0]), w_mem_kv[0].astype(BF16))

    yconv_s, u_s, q_s, k_s, v_s, logf_s, qm_s = _inproj_sample(
        xs, g1, w_cat, bf_pad, conv_w[0], state_conv[0, :, 0], state_conv[0, :, 1],
        n_heads=n_heads, head_dim=head_dim, mem_width=mem_width)
    per_head = lambda a, nh: a.reshape(Bd, nh, head_dim)

    yfox, yfox_s = _fox(qaugt, kaug, vtb, page_table,
                        _lane_broadcast(per_head(q_s, n_heads), page), _lane_broadcast(per_head(k_s, n_heads), page),
                        _lane_broadcast(per_head(v_s, n_heads), page), _lane_broadcast(logf_s[:, :n_heads], page),
                        jnp.transpose(cache_k[0], (0, 2, 3, 1)), jnp.transpose(cache_v[0], (0, 2, 3, 1)),
                        jnp.transpose(cache_logf[0], (0, 2, 1)), head_dim=head_dim)
    y_prompt = _post(x_prompt.reshape(B * S, D), yconv.reshape(B * S, cw_width), yfox.reshape(B * S, fox_width),
                     qm.reshape(B * S, mem_width), *post_args, tm=ROW_TILE, mem_kv=(mk, mv))
    ymem_s = _mem_attn_sample(_lane_broadcast(per_head(qm_s, mem_heads), mem_len),
                              jnp.transpose(cache_mem_k[0], (0, 2, 3, 1)), jnp.transpose(cache_mem_v[0], (0, 2, 3, 1)))
    y_sample = _post(xs, yconv_s, yfox_s.reshape(Bd, fox_width), ymem_s.reshape(Bd, mem_width), *post_args, tm=Bd)

    return (y_prompt.reshape(B, S, D), y_sample.reshape(Bd, 1, D),
            k_p.reshape(1, B, S, n_heads, head_dim), v_p.reshape(1, B, S, n_heads, head_dim), logf_p[None],
            utail[None, :, SUBLANES - (conv_k - 1):],
            mk.reshape(1, B, mem_len, mem_heads, head_dim), mv.reshape(1, B, mem_len, mem_heads, head_dim),
            k_s.reshape(1, Bd, 1, n_heads, head_dim), v_s.reshape(1, Bd, 1, n_heads, head_dim),
            logf_s[None, :, None, :n_heads], jnp.stack([state_conv[0, :, 1], u_s], axis=1)[None])
```
